```python
import jax, jax.numpy as jnp
from jax import lax
import numpy as np

D_MODEL = 4096
BATCH = 4
SEQ = 4096
DEPTH = 1

HEAD_DIM = 128
A_HEADS = 16
A_KV_HEADS = 4
A_GROUP = A_HEADS // A_KV_HEADS
B_PATTERNS = ((128, 1), (512, 4), (2048, 16))
B_HEADS_PER_GROUP = 8
B_N_GROUPS = len(B_PATTERNS)
B_HEADS = B_N_GROUPS * B_HEADS_PER_GROUP
A_Q_W = A_HEADS * HEAD_DIM
A_KV_W = A_KV_HEADS * HEAD_DIM
A_OUT_W = A_Q_W
B_QKV_W = 3 * B_HEADS * HEAD_DIM
B_OUT_W = B_HEADS_PER_GROUP * HEAD_DIM
GATE_W = 2 * D_MODEL
IN_W = A_Q_W + 2 * A_KV_W + B_QKV_W + GATE_W
D_FF = 11008
GRID_W = 64
ROPE_THETA = 10000.0
ROPE_AXIS_DIM = HEAD_DIM // 2
Q_BLOCK = 128
RMS_EPS = 1e-6
NEG_INF = -1e30

kernel_name = 'hybrid_gated_gqa_dilated_macaron'


def rms_norm(x, g):
    x32 = x.astype(jnp.float32)
    y = x32 * lax.rsqrt(jnp.mean(x32 * x32, axis=-1, keepdims=True) + RMS_EPS)
    return (y * g.astype(jnp.float32)).astype(x.dtype)


def swiglu(x, w_gate, w_up, w_down):
    return (jax.nn.silu(x @ w_gate) * (x @ w_up)) @ w_down


def axial_rope_angles(T):
    rows = T // GRID_W
    row_ids = jnp.repeat(jnp.arange(rows), GRID_W).astype(jnp.float32)
    col_ids = jnp.tile(jnp.arange(GRID_W), rows).astype(jnp.float32)
    inv = ROPE_THETA ** (-jnp.arange(0, ROPE_AXIS_DIM, 2, dtype=jnp.float32) / ROPE_AXIS_DIM)
    return row_ids[:, None] * inv[None, :], col_ids[:, None] * inv[None, :]


def _rotate_half(x, ang):
    half = x.shape[-1] // 2
    x1, x2 = x[..., :half], x[..., half:]
    c = jnp.cos(ang)[:, None, :]
    s = jnp.sin(ang)[:, None, :]
    return jnp.concatenate([x1 * c - x2 * s, x1 * s + x2 * c], axis=-1)


def apply_axial_rope(x, ang_r, ang_c):
    x32 = x.astype(jnp.float32)
    out = jnp.concatenate([_rotate_half(x32[..., :ROPE_AXIS_DIM], ang_r),
                           _rotate_half(x32[..., ROPE_AXIS_DIM:], ang_c)], axis=-1)
    return out.astype(x.dtype)


def global_gqa_attention(q, k, v):
    bsz, T = q.shape[0], q.shape[1]
    nb = T // Q_BLOCK
    qb = q.reshape(bsz, nb, Q_BLOCK, A_KV_HEADS, A_GROUP, HEAD_DIM).transpose(1, 0, 2, 3, 4, 5)
    scale = HEAD_DIM ** -0.5

    def block(qblk):
        s = jnp.einsum('bqkgd,bskd->bkgqs', qblk, k, preferred_element_type=jnp.float32) * scale
        p = jax.nn.softmax(s, axis=-1).astype(v.dtype)
        return jnp.einsum('bkgqs,bskd->bqkgd', p, v)

    out = lax.map(block, qb)
    return out.transpose(1, 0, 2, 3, 4, 5).reshape(bsz, T, A_OUT_W)


def dilated_window_attention(q, k, v, window, dilation, slopes):
    bsz, T, H, hd = q.shape
    half_span = window // 2
    n_side = half_span // dilation
    offsets = jnp.arange(-n_side, n_side + 1) * dilation
    k_pad = jnp.pad(k, ((0, 0), (half_span, half_span), (0, 0), (0, 0)))
    v_pad = jnp.pad(v, ((0, 0), (half_span, half_span), (0, 0), (0, 0)))
    bias = -slopes.astype(jnp.float32)[:, None] * jnp.abs(offsets).astype(jnp.float32)[None, :]
    nb = T // Q_BLOCK
    qb = q.reshape(bsz, nb, Q_BLOCK, H, hd).transpose(1, 0, 2, 3, 4)
    starts = jnp.arange(nb) * Q_BLOCK
    scale = hd ** -0.5

    def block(args):
        qblk, s0 = args
        pos = s0 + jnp.arange(Q_BLOCK)
        key_pos = pos[:, None] + offsets[None, :]
        valid = (key_pos >= 0) & (key_pos < T)
        idx = key_pos + half_span
        kb = jnp.take(k_pad, idx, axis=1)
        vb = jnp.take(v_pad, idx, axis=1)
        s = jnp.einsum('bqhd,bqjhd->bhqj', qblk, kb, preferred_element_type=jnp.float32) * scale
        s = jnp.where(valid[None, None], s + bias[:, None, :], NEG_INF)
        lse = jax.nn.logsumexp(s, axis=-1)
        p = jnp.exp(s - lse[..., None]).astype(v.dtype)
        o = jnp.einsum('bhqj,bqjhd->bqhd', p, vb)
        return o, lse.transpose(0, 2, 1)

    o, lse = lax.map(block, (qb, starts))
    o = o.transpose(1, 0, 2, 3, 4).reshape(bsz, T, H, hd)
    lse = lse.transpose(1, 0, 2, 3).reshape(bsz, T, H)
    return o, lse


def hybrid_layer(h, g_ffn1, w1_gate, w1_up, w1_down, g_mix, w_in, q_norm_a, k_norm_a,
                 w_branch_a, w_branch_b, w_out, g_ffn2, w2_gate, w2_up, w2_down):
    bsz, T, _ = h.shape
    h = h + 0.5 * swiglu(rms_norm(h, g_ffn1), w1_gate, w1_up, w1_down)

    u = rms_norm(h, g_mix)
    proj = u @ w_in
    c0 = A_Q_W
    c1 = c0 + A_KV_W
    c2 = c1 + A_KV_W
    c3 = c2 + B_QKV_W
    q_a = proj[..., :c0].reshape(bsz, T, A_HEADS, HEAD_DIM)
    k_a = proj[..., c0:c1].reshape(bsz, T, A_KV_HEADS, HEAD_DIM)
    v_a = proj[..., c1:c2].reshape(bsz, T, A_KV_HEADS, HEAD_DIM)
    qkv_b = proj[..., c2:c3].reshape(bsz, T, B_N_GROUPS, 3, B_HEADS_PER_GROUP, HEAD_DIM)
    gate_a = proj[..., c3:c3 + D_MODEL]
    gate_b = proj[..., c3 + D_MODEL:]

    ang_r, ang_c = axial_rope_angles(T)
    q_a = apply_axial_rope(rms_norm(q_a, q_norm_a), ang_r, ang_c)
    k_a = apply_axial_rope(rms_norm(k_a, k_norm_a), ang_r, ang_c)
    y_a = global_gqa_attention(q_a, k_a, v_a)

    slopes = jnp.exp2(-8.0 * jnp.arange(1, B_HEADS + 1, dtype=jnp.float32) / B_HEADS)
    outs, lses = [], []
    for gi, (window, dilation) in enumerate(B_PATTERNS):
        o, l = dilated_window_attention(qkv_b[:, :, gi, 0], qkv_b[:, :, gi, 1], qkv_b[:, :, gi, 2],
                                        window, dilation,
                                        slopes[gi * B_HEADS_PER_GROUP:(gi + 1) * B_HEADS_PER_GROUP])
        outs.append(o)
        lses.append(l)
    alpha = jax.nn.softmax(jnp.stack(lses, axis=0), axis=0)
    y_b = jnp.sum(alpha[..., None].astype(outs[0].dtype) * jnp.stack(outs, axis=0), axis=0)
    y_b = y_b.reshape(bsz, T, B_OUT_W)

    merged = jax.nn.sigmoid(gate_a) * (y_a @ w_branch_a) + jax.nn.sigmoid(gate_b) * (y_b @ w_branch_b)
    h = h + merged @ w_out

    h = h + 0.5 * swiglu(rms_norm(h, g_ffn2), w2_gate, w2_up, w2_down)
    return h


def setup_inputs(seed: int = 0) -> dict:
    key = jax.random.key(seed)
    ks = jax.random.split(key, 18)
    L = DEPTH

    def normal(k, shape, fan_in):
        return jax.random.normal(k, shape, jnp.float32) * (fan_in ** -0.5)

    def gain(k, shape):
        return 1.0 + 0.02 * jax.random.normal(k, shape, jnp.float32)

    return {
        'x': jax.random.normal(ks[0], (BATCH, SEQ, D_MODEL), jnp.float32),
        'g_ffn1': gain(ks[1], (L, D_MODEL)),
        'w1_gate': normal(ks[2], (L, D_MODEL, D_FF), D_MODEL),
        'w1_up': normal(ks[3], (L, D_MODEL, D_FF), D_MODEL),
        'w1_down': normal(ks[4], (L, D_FF, D_MODEL), D_FF),
        'g_mix': gain(ks[5], (L, D_MODEL)),
        'w_in': normal(ks[6], (L, D_MODEL, IN_W), D_MODEL),
        'q_norm_a': gain(ks[7], (L, HEAD_DIM)),
        'k_norm_a': gain(ks[8], (L, HEAD_DIM)),
        'w_branch_a': normal(ks[9], (L, A_OUT_W, D_MODEL), A_OUT_W),
        'w_branch_b': normal(ks[10], (L, B_OUT_W, D_MODEL), B_OUT_W),
        'w_out': normal(ks[11], (L, D_MODEL, D_MODEL), D_MODEL),
        'g_ffn2': gain(ks[12], (L, D_MODEL)),
        'w2_gate': normal(ks[13], (L, D_MODEL, D_FF), D_MODEL),
        'w2_up': normal(ks[14], (L, D_MODEL, D_FF), D_MODEL),
        'w2_down': normal(ks[15], (L, D_FF, D_MODEL), D_FF),
        'g_final': gain(ks[16], (D_MODEL,)),
    }


def reference(x, g_ffn1, w1_gate, w1_up, w1_down, g_mix, w_in, q_norm_a, k_norm_a,
              w_branch_a, w_branch_b, w_out, g_ffn2, w2_gate, w2_up, w2_down, g_final):
    h = x
    for l in range(DEPTH):
        h = hybrid_layer(h, g_ffn1[l], w1_gate[l], w1_up[l], w1_down[l], g_mix[l], w_in[l],
                         q_norm_a[l], k_norm_a[l], w_branch_a[l], w_branch_b[l], w_out[l],
                         g_ffn2[l], w2_gate[l], w2_up[l], w2_down[l])
    return rms_norm(h, g_final)
```

```python
import functools

import jax
import jax.numpy as jnp
from jax import lax
from jax.experimental import pallas as pl
from jax.experimental.pallas import tpu as pltpu

HEAD_DIM = 128
A_HEADS = 16
A_KV_HEADS = 4
A_GROUP = A_HEADS // A_KV_HEADS
B_PATTERNS = ((128, 1), (512, 4), (2048, 16))
B_HEADS_PER_GROUP = 8
B_N_GROUPS = len(B_PATTERNS)
B_HEADS = B_N_GROUPS * B_HEADS_PER_GROUP
A_Q_W = A_HEADS * HEAD_DIM
A_KV_W = A_KV_HEADS * HEAD_DIM
B_QKV_W = 3 * B_HEADS * HEAD_DIM
B_OUT_W = B_HEADS_PER_GROUP * HEAD_DIM
GRID_W = 64
ROPE_THETA = 10000.0
ROPE_AXIS_DIM = HEAD_DIM // 2
RMS_EPS = 1e-6
NEG_INF = -1e30

DOWN_CHUNK = 512
MIB = 1024 * 1024
BF16 = jnp.bfloat16
F32 = jnp.float32


def _cparams(sem, vmem_mib):
    return pltpu.CompilerParams(dimension_semantics=sem, vmem_limit_bytes=vmem_mib * MIB)


def _tile(n, pref):
    t = min(n, pref)
    while n % t:
        t //= 2
    return t


def _rms(x, gain):
    y = x * lax.rsqrt(jnp.mean(x * x, axis=-1, keepdims=True) + RMS_EPS)
    return y * gain


def _dot(a, b):
    return jnp.dot(a, b, preferred_element_type=F32)


def _ffn_kernel(x_ref, gin_ref, wg_ref, wu_ref, wd_ref, gout_ref, *rest, emit_hidden):
    if emit_hidden:
        h_ref, y_ref, xn_ref = rest
        acc_ref = h_ref
    else:
        y_ref, xn_ref = rest
        acc_ref = y_ref
    j = pl.program_id(1)

    @pl.when(j == 0)
    def _():
        x = x_ref[...]
        xn_ref[...] = _rms(x, gin_ref[...]).astype(BF16)
        acc_ref[...] = x

    xn = xn_ref[...]
    gate = _dot(xn, wg_ref[...])
    up = _dot(xn, wu_ref[...])
    act = (jax.nn.silu(gate) * up * 0.5).astype(BF16)
    for c in range(0, acc_ref.shape[1], DOWN_CHUNK):
        acc_ref[:, c:c + DOWN_CHUNK] += _dot(act, wd_ref[:, c:c + DOWN_CHUNK])

    @pl.when(j == pl.num_programs(1) - 1)
    def _():
        y_ref[...] = _rms(acc_ref[...], gout_ref[...]).astype(y_ref.dtype)


def _ffn(x, g_in, wg, wu, wd, g_out, *, emit_hidden, bm, tf):
    n, d = x.shape
    f = wg.shape[1]
    grid = (n // bm, f // tf)
    row = pl.BlockSpec((bm, d), lambda i, j: (i, 0))
    vec = pl.BlockSpec((1, d), lambda i, j: (0, 0))
    in_specs = [pl.BlockSpec((bm, d), lambda i, j: (i, 0), pipeline_mode=pl.Buffered(1)), vec,
                pl.BlockSpec((d, tf), lambda i, j: (0, j)),
                pl.BlockSpec((d, tf), lambda i, j: (0, j)),
                pl.BlockSpec((tf, d), lambda i, j: (j, 0)),
                vec]
    if emit_hidden:
        out_shape = (jax.ShapeDtypeStruct((n, d), F32), jax.ShapeDtypeStruct((n, d), BF16))
        out_specs = (row, row)
    else:
        out_shape = jax.ShapeDtypeStruct((n, d), F32)
        out_specs = row
    return pl.pallas_call(
        functools.partial(_ffn_kernel, emit_hidden=emit_hidden),
        grid=grid, in_specs=in_specs, out_specs=out_specs, out_shape=out_shape,
        scratch_shapes=[pltpu.VMEM((bm, d), BF16)],
        compiler_params=_cparams(("parallel", "arbitrary"), 58),
        name="ffn_hidden" if emit_hidden else "ffn_final",
    )(x, g_in.reshape(1, d), wg, wu, wd, g_out.reshape(1, d))


def _mm_kernel(a_ref, b_ref, o_ref):
    o_ref[...] = _dot(a_ref[...], b_ref[...]).astype(o_ref.dtype)


def _matmul(a, b, *, bm, bn, out_dtype):
    m, k = a.shape
    n = b.shape[1]
    return pl.pallas_call(
        _mm_kernel,
        grid=(m // bm, n // bn),
        in_specs=[pl.BlockSpec((bm, k), lambda i, j: (i, 0)),
                  pl.BlockSpec((k, bn), lambda i, j: (0, j))],
        out_specs=pl.BlockSpec((bm, bn), lambda i, j: (i, j)),
        out_shape=jax.ShapeDtypeStruct((m, n), out_dtype),
        compiler_params=_cparams(("parallel", "arbitrary"), 48),
        name="in_proj",
    )(a, b)


def _rope(y, cos, sin_signed, first_half):
    partner = jnp.where(first_half, pltpu.roll(y, HEAD_DIM - 32, axis=1), pltpu.roll(y, 32, axis=1))
    return y * cos + partner * sin_signed


def _prep_kernel(q_ref, k_ref, cos_ref, sin_ref, qg_ref, kg_ref, qo_ref, kt_ref):
    cos = cos_ref[...]
    sin = sin_ref[...]
    lane = lax.broadcasted_iota(jnp.int32, cos.shape, 1)
    first_half = (lane % 64) < 32
    scale = HEAD_DIM ** -0.5
    for h in range(A_HEADS):
        sl = slice(h * HEAD_DIM, (h + 1) * HEAD_DIM)
        y = _rms(q_ref[0, :, sl].astype(F32), qg_ref[...])
        qo_ref[0, :, sl] = (_rope(y, cos, sin, first_half) * scale).astype(BF16)
    for h in range(A_KV_HEADS):
        sl = slice(h * HEAD_DIM, (h + 1) * HEAD_DIM)
        y = _rms(k_ref[0, :, sl].astype(F32), kg_ref[...])
        kt_ref[0, sl, :] = _rope(y, cos, sin, first_half).T.astype(BF16)


def _rope_tables(t):
    rows = t // GRID_W
    row_ids = jnp.repeat(jnp.arange(rows), GRID_W).astype(F32)
    col_ids = jnp.tile(jnp.arange(GRID_W), rows).astype(F32)
    inv = ROPE_THETA ** (-jnp.arange(0, ROPE_AXIS_DIM, 2, dtype=F32) / ROPE_AXIS_DIM)
    ang_r = row_ids[:, None] * inv[None, :]
    ang_c = col_ids[:, None] * inv[None, :]
    cos = jnp.concatenate([jnp.cos(ang_r)] * 2 + [jnp.cos(ang_c)] * 2, axis=-1)
    sin = jnp.concatenate([-jnp.sin(ang_r), jnp.sin(ang_r), -jnp.sin(ang_c), jnp.sin(ang_c)], axis=-1)
    return cos, sin


def _prep(proj, q_gain, k_gain, *, tt):
    b, t, _ = proj.shape
    cos, sin = _rope_tables(t)
    tab = pl.BlockSpec((tt, HEAD_DIM), lambda bi, i: (i, 0))
    vec = pl.BlockSpec((1, HEAD_DIM), lambda bi, i: (0, 0))
    return pl.pallas_call(
        _prep_kernel,
        grid=(b, t // tt),
        in_specs=[pl.BlockSpec((1, tt, A_Q_W), lambda bi, i: (bi, i, 0)),
                  pl.BlockSpec((1, tt, A_KV_W), lambda bi, i: (bi, i, A_Q_W // A_KV_W)),
                  tab, tab, vec, vec],
        out_specs=(pl.BlockSpec((1, tt, A_Q_W), lambda bi, i: (bi, i, 0)),
                   pl.BlockSpec((1, A_KV_W, tt), lambda bi, i: (bi, 0, i))),
        out_shape=(jax.ShapeDtypeStruct((b, t, A_Q_W), BF16),
                   jax.ShapeDtypeStruct((b, A_KV_W, t), BF16)),
        compiler_params=_cparams(("parallel", "parallel"), 32),
        name="qk_prep",
    )(proj, proj, cos, sin, q_gain.reshape(1, HEAD_DIM), k_gain.reshape(1, HEAD_DIM))


def _attn_a_kernel(q_ref, kt_ref, v_ref, o_ref):
    kt = kt_ref[0]
    v = v_ref[0]
    for g in range(A_GROUP):
        sl = slice(g * HEAD_DIM, (g + 1) * HEAD_DIM)
        s = _dot(q_ref[0, :, sl], kt)
        p = jnp.exp(s - jnp.max(s, axis=-1, keepdims=True))
        denom = jnp.sum(p, axis=-1, keepdims=True)
        o = _dot(p.astype(BF16), v) / denom
        o_ref[0, :, sl] = o.astype(o_ref.dtype)


def _attn_a(qp, kt, proj, *, tq):
    b, t, _ = qp.shape
    gw = A_GROUP * HEAD_DIM
    v_blk0 = (A_Q_W + A_KV_W) // HEAD_DIM
    return pl.pallas_call(
        _attn_a_kernel,
        grid=(b, A_KV_HEADS, t // tq),
        in_specs=[pl.BlockSpec((1, tq, gw), lambda bi, kv, i: (bi, i, kv)),
                  pl.BlockSpec((1, HEAD_DIM, t), lambda bi, kv, i: (bi, kv, 0)),
                  pl.BlockSpec((1, t, HEAD_DIM), lambda bi, kv, i: (bi, 0, v_blk0 + kv))],
        out_specs=pl.BlockSpec((1, tq, gw), lambda bi, kv, i: (bi, i, kv)),
        out_shape=jax.ShapeDtypeStruct((b, t, A_Q_W), BF16),
        compiler_params=_cparams(("parallel", "parallel", "arbitrary"), 48),
        name="attn_a",
    )(qp, kt, proj)


def _attn_b_kernel(slopes_ref, q_ref, k_ref, v_ref, o_ref, lse_ref, *, group, dilation, half, tq, kw):
    length = q_ref.shape[1]
    slope = slopes_ref[group * B_HEADS_PER_GROUP + pl.program_id(1)]
    scale = HEAD_DIM ** -0.5
    row = lax.broadcasted_iota(jnp.int32, (tq, kw), 0)
    col = lax.broadcasted_iota(jnp.int32, (tq, kw), 1)

    def tile(i, carry):
        m0 = pl.multiple_of(i * tq, tq)
        ks = pl.multiple_of(jnp.clip(m0 - half, 0, length - kw), half)
        q = q_ref[0, pl.ds(m0, tq), :]
        k = k_ref[0, pl.ds(ks, kw), :]
        v = v_ref[0, pl.ds(ks, kw), :]
        s = lax.dot_general(q, k, (((1,), (1,)), ((), ())), preferred_element_type=F32) * scale
        dist = jnp.abs(col - row + (ks - m0))
        bias = -slope * (dist * dilation).astype(F32)
        s = jnp.where(dist <= half, s + bias, NEG_INF)
        m = jnp.max(s, axis=-1, keepdims=True)
        p = jnp.exp(s - m)
        denom = jnp.sum(p, axis=-1, keepdims=True)
        o = _dot(p.astype(BF16), v) / denom
        o_ref[0, pl.ds(m0, tq), :] = o.astype(o_ref.dtype)
        lse_ref[0, pl.ds(m0, tq), :] = jnp.broadcast_to(m + jnp.log(denom), (tq, HEAD_DIM))
        return carry

    lax.fori_loop(0, length // tq, tile, 0)


def _attn_b(proj, slopes, group):
    window, dilation = B_PATTERNS[group]
    b, t, in_w = proj.shape
    length = t // dilation
    half = (window // 2) // dilation
    tq = min(128, length)
    kw = min(length, tq + 2 * half)
    cls = proj.reshape(b, length, dilation * in_w)
    blocks_per_tok = in_w // HEAD_DIM
    base = A_Q_W // HEAD_DIM + 2 * A_KV_HEADS + group * 3 * B_HEADS_PER_GROUP

    def in_spec(part):
        off = base + part * B_HEADS_PER_GROUP
        return pl.BlockSpec((1, length, HEAD_DIM),
                            lambda bi, h, r, s: (bi, 0, r * blocks_per_tok + off + h))

    out_spec = pl.BlockSpec((1, length, HEAD_DIM), lambda bi, h, r, s: (bi, 0, r * B_HEADS_PER_GROUP + h))
    o, lse = pl.pallas_call(
        functools.partial(_attn_b_kernel, group=group, dilation=dilation, half=half, tq=tq, kw=kw),
        grid_spec=pltpu.PrefetchScalarGridSpec(
            num_scalar_prefetch=1,
            grid=(b, B_HEADS_PER_GROUP, dilation),
            in_specs=[in_spec(0), in_spec(1), in_spec(2)],
            out_specs=(out_spec, out_spec)),
        out_shape=(jax.ShapeDtypeStruct((b, length, dilation * B_OUT_W), BF16),
                   jax.ShapeDtypeStruct((b, length, dilation * B_OUT_W), F32)),
        compiler_params=_cparams(("parallel", "parallel", "parallel"), 32),
        name=f"attn_b{group}",
    )(slopes, cls, cls, cls)
    return o.reshape(b * t, B_OUT_W), lse.reshape(b * t, B_OUT_W)


def _branch_kernel(ya_ref, o0_ref, o1_ref, o2_ref, l0_ref, l1_ref, l2_ref, wa_ref, wb_ref,
                   ga_ref, gb_ref, out_ref, yb_ref):
    @pl.when(pl.program_id(1) == 0)
    def _():
        l0, l1, l2 = l0_ref[...], l1_ref[...], l2_ref[...]
        m = jnp.maximum(jnp.maximum(l0, l1), l2)
        e0, e1, e2 = jnp.exp(l0 - m), jnp.exp(l1 - m), jnp.exp(l2 - m)
        tot = e0 + e1 + e2
        yb = ((e0 / tot) * o0_ref[...].astype(F32) + (e1 / tot) * o1_ref[...].astype(F32)
              + (e2 / tot) * o2_ref[...].astype(F32))
        yb_ref[...] = yb.astype(BF16)

    a = _dot(ya_ref[...], wa_ref[...])
    bb = _dot(yb_ref[...], wb_ref[...])
    merged = jax.nn.sigmoid(ga_ref[...].astype(F32)) * a + jax.nn.sigmoid(gb_ref[...].astype(F32)) * bb
    out_ref[...] = merged.astype(out_ref.dtype)


def _branch(ya, outs, lses, wa, wb, proj2d, *, bm, bn):
    n, d = ya.shape[0], wa.shape[1]
    gate0 = (A_Q_W + 2 * A_KV_W + B_QKV_W) // bn
    row = lambda w: pl.BlockSpec((bm, w), lambda i, j: (i, 0))
    return pl.pallas_call(
        _branch_kernel,
        grid=(n // bm, d // bn),
        in_specs=[row(A_Q_W)] + [row(B_OUT_W)] * 6
                 + [pl.BlockSpec((A_Q_W, bn), lambda i, j: (0, j)),
                    pl.BlockSpec((B_OUT_W, bn), lambda i, j: (0, j)),
                    pl.BlockSpec((bm, bn), lambda i, j: (i, gate0 + j)),
                    pl.BlockSpec((bm, bn), lambda i, j: (i, gate0 + d // bn + j))],
        out_specs=pl.BlockSpec((bm, bn), lambda i, j: (i, j)),
        out_shape=jax.ShapeDtypeStruct((n, d), BF16),
        scratch_shapes=[pltpu.VMEM((bm, B_OUT_W), BF16)],
        compiler_params=_cparams(("parallel", "arbitrary"), 48),
        name="branch_merge",
    )(ya, *outs, *lses, wa, wb, proj2d, proj2d)


def _out_kernel(h_ref, a_ref, w_ref, o_ref):
    o_ref[...] = h_ref[...] + _dot(a_ref[...], w_ref[...])


def _out_proj(h, a, w, *, bm, bn):
    n, d = h.shape
    k = a.shape[1]
    return pl.pallas_call(
        _out_kernel,
        grid=(n // bm, d // bn),
        in_specs=[pl.BlockSpec((bm, bn), lambda i, j: (i, j)),
                  pl.BlockSpec((bm, k), lambda i, j: (i, 0)),
                  pl.BlockSpec((k, bn), lambda i, j: (0, j))],
        out_specs=pl.BlockSpec((bm, bn), lambda i, j: (i, j)),
        out_shape=jax.ShapeDtypeStruct((n, d), F32),
        compiler_params=_cparams(("parallel", "arbitrary"), 48),
        name="out_proj",
    )(h, a, w)


def _layer(h, g_ffn1, w1_gate, w1_up, w1_down, g_mix, w_in, q_norm_a, k_norm_a,
           w_branch_a, w_branch_b, w_out, g_ffn2, w2_gate, w2_up, w2_down, g_next, *, bsz, last):
    n, d = h.shape
    t = n // bsz
    c = lambda w: w.astype(BF16)
    bm = _tile(n, 512)
    tf = _tile(w1_gate.shape[1], 256)

    h, u = _ffn(h, g_ffn1, c(w1_gate), c(w1_up), c(w1_down), g_mix, emit_hidden=True, bm=bm, tf=tf)
    proj2d = _matmul(u, c(w_in), bm=_tile(n, 1024), bn=_tile(w_in.shape[1], 1024), out_dtype=BF16)
    proj = proj2d.reshape(bsz, t, -1)

    qp, kt = _prep(proj, q_norm_a, k_norm_a, tt=_tile(t, 512))
    ya = _attn_a(qp, kt, proj, tq=_tile(t, 256)).reshape(n, A_Q_W)

    slopes = jnp.exp2(-8.0 * jnp.arange(1, B_HEADS + 1, dtype=F32) / B_HEADS)
    outs, lses = zip(*[_attn_b(proj, slopes, g) for g in range(B_N_GROUPS)])

    merged = _branch(ya, outs, lses, c(w_branch_a), c(w_branch_b), proj2d, bm=bm, bn=_tile(d, 512))
    h = _out_proj(h, merged, c(w_out), bm=_tile(n, 1024), bn=_tile(d, 512))

    if last:
        return _ffn(h, g_ffn2, c(w2_gate), c(w2_up), c(w2_down), g_next, emit_hidden=False, bm=bm, tf=tf)
    return _ffn(h, g_ffn2, c(w2_gate), c(w2_up), c(w2_down), g_next, emit_hidden=True, bm=bm, tf=tf)[0]


def kernel(x, g_ffn1, w1_gate, w1_up, w1_down, g_mix, w_in, q_norm_a, k_norm_a, w_branch_a, w_branch_b,
           w_out, g_ffn2, w2_gate, w2_up, w2_down, g_final):
    bsz, t, d = x.shape
    depth = g_ffn1.shape[0]
    h = x.reshape(bsz * t, d)
    for l in range(depth):
        last = l == depth - 1
        h = _layer(h, g_ffn1[l], w1_gate[l], w1_up[l], w1_down[l], g_mix[l], w_in[l], q_norm_a[l],
                   k_norm_a[l], w_branch_a[l], w_branch_b[l], w_out[l], g_ffn2[l], w2_gate[l], w2_up[l],
                   w2_down[l], g_final if last else g_ffn1[l + 1], bsz=bsz, last=last)
    return h.reshape(bsz, t, d)
```

```python
import functools

import jax
import jax.numpy as jnp
from jax import lax
from jax.experimental import pallas as pl
from jax.experimental.pallas import tpu as pltpu

HEAD_DIM = 128
A_HEADS = 16
A_KV_HEADS = 4
A_GROUP = A_HEADS // A_KV_HEADS
B_PATTERNS = ((128, 1), (512, 4), (2048, 16))
B_HEADS_PER_GROUP = 8
B_N_GROUPS = len(B_PATTERNS)
B_HEADS = B_N_GROUPS * B_HEADS_PER_GROUP
A_Q_W = A_HEADS * HEAD_DIM
A_KV_W = A_KV_HEADS * HEAD_DIM
B_QKV_W = 3 * B_HEADS * HEAD_DIM
B_OUT_W = B_HEADS_PER_GROUP * HEAD_DIM
GRID_W = 64
ROPE_THETA = 10000.0
ROPE_AXIS_DIM = HEAD_DIM // 2
RMS_EPS = 1e-6
NEG_INF = -1e30

DOWN_CHUNK = 512
MIB = 1024 * 1024
BF16 = jnp.bfloat16
F32 = jnp.float32


def _cparams(sem, vmem_mib):
    return pltpu.CompilerParams(dimension_semantics=sem, vmem_limit_bytes=vmem_mib * MIB)


def _tile(n, pref):
    t = min(n, pref)
    while n % t:
        t //= 2
    return t


def _rms(x, gain):
    y = x * lax.rsqrt(jnp.mean(x * x, axis=-1, keepdims=True) + RMS_EPS)
    return y * gain


def _dot(a, b):
    return jnp.dot(a, b, preferred_element_type=F32)


def _ffn_kernel(x_ref, gin_ref, wgu_ref, wd_ref, gout_ref, *rest, emit_hidden):
    if emit_hidden:
        h_ref, y_ref, xn_ref = rest
        acc_ref = h_ref
    else:
        y_ref, xn_ref = rest
        acc_ref = y_ref
    j = pl.program_id(1)

    @pl.when(j == 0)
    def _():
        x = x_ref[...]
        xn_ref[...] = _rms(x, gin_ref[...]).astype(BF16)
        acc_ref[...] = x

    tf = wd_ref.shape[0]
    gate_up = _dot(xn_ref[...], wgu_ref[...])
    act = (jax.nn.silu(gate_up[:, :tf]) * gate_up[:, tf:] * 0.5).astype(BF16)
    for c in range(0, acc_ref.shape[1], DOWN_CHUNK):
        acc_ref[:, c:c + DOWN_CHUNK] += _dot(act, wd_ref[:, c:c + DOWN_CHUNK])

    @pl.when(j == pl.num_programs(1) - 1)
    def _():
        y_ref[...] = _rms(acc_ref[...], gout_ref[...]).astype(y_ref.dtype)


def _gate_up_blocks(wg, wu, tf):
    d, f = wg.shape
    both = jnp.concatenate([wg.reshape(d, f // tf, tf), wu.reshape(d, f // tf, tf)], axis=2)
    return both.astype(BF16).transpose(1, 0, 2)


def _ffn(x, g_in, wgu, wd, g_out, *, emit_hidden, bm):
    n, d = x.shape
    nblk, _, tf2 = wgu.shape
    tf = tf2 // 2
    grid = (n // bm, nblk)
    row = pl.BlockSpec((bm, d), lambda i, j: (i, 0))
    vec = pl.BlockSpec((1, d), lambda i, j: (0, 0))
    in_specs = [pl.BlockSpec((bm, d), lambda i, j: (i, 0), pipeline_mode=pl.Buffered(1)), vec,
                pl.BlockSpec((None, d, tf2), lambda i, j: (j, 0, 0)),
                pl.BlockSpec((tf, d), lambda i, j: (j, 0)),
                vec]
    if emit_hidden:
        out_shape = (jax.ShapeDtypeStruct((n, d), F32), jax.ShapeDtypeStruct((n, d), BF16))
        out_specs = (row, row)
    else:
        out_shape = jax.ShapeDtypeStruct((n, d), F32)
        out_specs = row
    return pl.pallas_call(
        functools.partial(_ffn_kernel, emit_hidden=emit_hidden),
        grid=grid, in_specs=in_specs, out_specs=out_specs, out_shape=out_shape,
        scratch_shapes=[pltpu.VMEM((bm, d), BF16)],
        compiler_params=_cparams(("parallel", "arbitrary"), 58),
        name="ffn_hidden" if emit_hidden else "ffn_final",
    )(x, g_in.reshape(1, d), wgu, wd, g_out.reshape(1, d))


def _mm_kernel(a_ref, b_ref, o_ref):
    o_ref[...] = _dot(a_ref[...], b_ref[...]).astype(o_ref.dtype)


def _matmul(a, b, *, bm, bn, out_dtype):
    m, k = a.shape
    n = b.shape[1]
    return pl.pallas_call(
        _mm_kernel,
        grid=(m // bm, n // bn),
        in_specs=[pl.BlockSpec((bm, k), lambda i, j: (i, 0)),
                  pl.BlockSpec((k, bn), lambda i, j: (0, j))],
        out_specs=pl.BlockSpec((bm, bn), lambda i, j: (i, j)),
        out_shape=jax.ShapeDtypeStruct((m, n), out_dtype),
        compiler_params=_cparams(("parallel", "arbitrary"), 48),
        name="in_proj",
    )(a, b)


def _rope(y, cos, sin_signed, first_half):
    partner = jnp.where(first_half, pltpu.roll(y, HEAD_DIM - 32, axis=1), pltpu.roll(y, 32, axis=1))
    return y * cos + partner * sin_signed


def _prep_kernel(q_ref, k_ref, cos_ref, sin_ref, qg_ref, kg_ref, qo_ref, kt_ref):
    cos = cos_ref[...]
    sin = sin_ref[...]
    lane = lax.broadcasted_iota(jnp.int32, cos.shape, 1)
    first_half = (lane % 64) < 32
    scale = HEAD_DIM ** -0.5
    for h in range(A_HEADS):
        sl = slice(h * HEAD_DIM, (h + 1) * HEAD_DIM)
        y = _rms(q_ref[0, :, sl].astype(F32), qg_ref[...])
        qo_ref[0, :, sl] = (_rope(y, cos, sin, first_half) * scale).astype(BF16)
    for h in range(A_KV_HEADS):
        sl = slice(h * HEAD_DIM, (h + 1) * HEAD_DIM)
        y = _rms(k_ref[0, :, sl].astype(F32), kg_ref[...])
        kt_ref[0, sl, :] = _rope(y, cos, sin, first_half).T.astype(BF16)


def _rope_tables(t):
    rows = t // GRID_W
    row_ids = jnp.repeat(jnp.arange(rows), GRID_W).astype(F32)
    col_ids = jnp.tile(jnp.arange(GRID_W), rows).astype(F32)
    inv = ROPE_THETA ** (-jnp.arange(0, ROPE_AXIS_DIM, 2, dtype=F32) / ROPE_AXIS_DIM)
    ang_r = row_ids[:, None] * inv[None, :]
    ang_c = col_ids[:, None] * inv[None, :]
    cos = jnp.concatenate([jnp.cos(ang_r)] * 2 + [jnp.cos(ang_c)] * 2, axis=-1)
    sin = jnp.concatenate([-jnp.sin(ang_r), jnp.sin(ang_r), -jnp.sin(ang_c), jnp.sin(ang_c)], axis=-1)
    return cos, sin


def _prep(proj, q_gain, k_gain, *, tt):
    b, t, _ = proj.shape
    cos, sin = _rope_tables(t)
    tab = pl.BlockSpec((tt, HEAD_DIM), lambda bi, i: (i, 0))
    vec = pl.BlockSpec((1, HEAD_DIM), lambda bi, i: (0, 0))
    return pl.pallas_call(
        _prep_kernel,
        grid=(b, t // tt),
        in_specs=[pl.BlockSpec((1, tt, A_Q_W), lambda bi, i: (bi, i, 0)),
                  pl.BlockSpec((1, tt, A_KV_W), lambda bi, i: (bi, i, A_Q_W // A_KV_W)),
                  tab, tab, vec, vec],
        out_specs=(pl.BlockSpec((1, tt, A_Q_W), lambda bi, i: (bi, i, 0)),
                   pl.BlockSpec((1, A_KV_W, tt), lambda bi, i: (bi, 0, i))),
        out_shape=(jax.ShapeDtypeStruct((b, t, A_Q_W), BF16),
                   jax.ShapeDtypeStruct((b, A_KV_W, t), BF16)),
        compiler_params=_cparams(("parallel", "parallel"), 32),
        name="qk_prep",
    )(proj, proj, cos, sin, q_gain.reshape(1, HEAD_DIM), k_gain.reshape(1, HEAD_DIM))


def _attn_a_kernel(q_ref, kt_ref, v_ref, o_ref):
    kt = kt_ref[0]
    v = v_ref[0]
    for g in range(A_GROUP):
        sl = slice(g * HEAD_DIM, (g + 1) * HEAD_DIM)
        s = _dot(q_ref[0, :, sl], kt)
        p = jnp.exp(s - jnp.max(s, axis=-1, keepdims=True))
        denom = jnp.sum(p, axis=-1, keepdims=True)
        o = _dot(p.astype(BF16), v) / denom
        o_ref[0, :, sl] = o.astype(o_ref.dtype)


def _attn_a(qp, kt, proj, *, tq):
    b, t, _ = qp.shape
    gw = A_GROUP * HEAD_DIM
    v_blk0 = (A_Q_W + A_KV_W) // HEAD_DIM
    return pl.pallas_call(
        _attn_a_kernel,
        grid=(b, A_KV_HEADS, t // tq),
        in_specs=[pl.BlockSpec((1, tq, gw), lambda bi, kv, i: (bi, i, kv)),
                  pl.BlockSpec((1, HEAD_DIM, t), lambda bi, kv, i: (bi, kv, 0)),
                  pl.BlockSpec((1, t, HEAD_DIM), lambda bi, kv, i: (bi, 0, v_blk0 + kv))],
        out_specs=pl.BlockSpec((1, tq, gw), lambda bi, kv, i: (bi, i, kv)),
        out_shape=jax.ShapeDtypeStruct((b, t, A_Q_W), BF16),
        compiler_params=_cparams(("parallel", "parallel", "arbitrary"), 48),
        name="attn_a",
    )(qp, kt, proj)


def _attn_b_kernel(slopes_ref, q_ref, k_ref, v_ref, o_ref, lse_ref, *, group, dilation, half, tq, kw):
    length = q_ref.shape[1]
    slope = slopes_ref[group * B_HEADS_PER_GROUP + pl.program_id(1)]
    scale = HEAD_DIM ** -0.5
    row = lax.broadcasted_iota(jnp.int32, (tq, kw), 0)
    col = lax.broadcasted_iota(jnp.int32, (tq, kw), 1)

    def tile(i, carry):
        m0 = pl.multiple_of(i * tq, tq)
        ks = pl.multiple_of(jnp.clip(m0 - half, 0, length - kw), half)
        q = q_ref[0, pl.ds(m0, tq), :]
        k = k_ref[0, pl.ds(ks, kw), :]
        v = v_ref[0, pl.ds(ks, kw), :]
        s = lax.dot_general(q, k, (((1,), (1,)), ((), ())), preferred_element_type=F32) * scale
        dist = jnp.abs(col - row + (ks - m0))
        bias = -slope * (dist * dilation).astype(F32)
        s = jnp.where(dist <= half, s + bias, NEG_INF)
        m = jnp.max(s, axis=-1, keepdims=True)
        p = jnp.exp(s - m)
        denom = jnp.sum(p, axis=-1, keepdims=True)
        o = _dot(p.astype(BF16), v) / denom
        o_ref[0, pl.ds(m0, tq), :] = o.astype(o_ref.dtype)
        lse_ref[0, pl.ds(m0, tq), :] = jnp.broadcast_to(m + jnp.log(denom), (tq, HEAD_DIM))
        return carry

    lax.fori_loop(0, length // tq, tile, 0)


def _attn_b(proj, slopes, group):
    window, dilation = B_PATTERNS[group]
    b, t, in_w = proj.shape
    length = t // dilation
    half = (window // 2) // dilation
    tq = min(128, length)
    kw = min(length, tq + 2 * half)
    gw = 3 * B_OUT_W
    c0 = A_Q_W + 2 * A_KV_W + group * gw
    if dilation == 1:
        cls, blocks_per_tok, base = proj, in_w // HEAD_DIM, c0 // HEAD_DIM
    else:
        cls = proj[:, :, c0:c0 + gw].reshape(b, length, dilation * gw)
        blocks_per_tok, base = gw // HEAD_DIM, 0

    def in_spec(part):
        off = base + part * B_HEADS_PER_GROUP
        return pl.BlockSpec((1, length, HEAD_DIM),
                            lambda bi, h, r, s: (bi, 0, r * blocks_per_tok + off + h))

    out_spec = pl.BlockSpec((1, length, HEAD_DIM), lambda bi, h, r, s: (bi, 0, r * B_HEADS_PER_GROUP + h))
    o, lse = pl.pallas_call(
        functools.partial(_attn_b_kernel, group=group, dilation=dilation, half=half, tq=tq, kw=kw),
        grid_spec=pltpu.PrefetchScalarGridSpec(
            num_scalar_prefetch=1,
            grid=(b, B_HEADS_PER_GROUP, dilation),
            in_specs=[in_spec(0), in_spec(1), in_spec(2)],
            out_specs=(out_spec, out_spec)),
        out_shape=(jax.ShapeDtypeStruct((b, length, dilation * B_OUT_W), BF16),
                   jax.ShapeDtypeStruct((b, length, dilation * B_OUT_W), F32)),
        compiler_params=_cparams(("parallel", "parallel", "parallel"), 32),
        name=f"attn_b{group}",
    )(slopes, cls, cls, cls)
    return o.reshape(b * t, B_OUT_W), lse.reshape(b * t, B_OUT_W)


def _branch_kernel(ya_ref, o0_ref, o1_ref, o2_ref, l0_ref, l1_ref, l2_ref, wa_ref, wb_ref,
                   ga_ref, gb_ref, out_ref, yb_ref):
    @pl.when(pl.program_id(1) == 0)
    def _():
        l0, l1, l2 = l0_ref[...], l1_ref[...], l2_ref[...]
        m = jnp.maximum(jnp.maximum(l0, l1), l2)
        e0, e1, e2 = jnp.exp(l0 - m), jnp.exp(l1 - m), jnp.exp(l2 - m)
        tot = e0 + e1 + e2
        yb = ((e0 / tot) * o0_ref[...].astype(F32) + (e1 / tot) * o1_ref[...].astype(F32)
              + (e2 / tot) * o2_ref[...].astype(F32))
        yb_ref[...] = yb.astype(BF16)

    a = _dot(ya_ref[...], wa_ref[...])
    bb = _dot(yb_ref[...], wb_ref[...])
    merged = jax.nn.sigmoid(ga_ref[...].astype(F32)) * a + jax.nn.sigmoid(gb_ref[...].astype(F32)) * bb
    out_ref[...] = merged.astype(out_ref.dtype)


def _branch(ya, outs, lses, wa, wb, proj2d, *, bm, bn):
    n, d = ya.shape[0], wa.shape[1]
    gate0 = (A_Q_W + 2 * A_KV_W + B_QKV_W) // bn
    row = lambda w: pl.BlockSpec((bm, w), lambda i, j: (i, 0))
    return pl.pallas_call(
        _branch_kernel,
        grid=(n // bm, d // bn),
        in_specs=[row(A_Q_W)] + [row(B_OUT_W)] * 6
                 + [pl.BlockSpec((A_Q_W, bn), lambda i, j: (0, j)),
                    pl.BlockSpec((B_OUT_W, bn), lambda i, j: (0, j)),
                    pl.BlockSpec((bm, bn), lambda i, j: (i, gate0 + j)),
                    pl.BlockSpec((bm, bn), lambda i, j: (i, gate0 + d // bn + j))],
        out_specs=pl.BlockSpec((bm, bn), lambda i, j: (i, j)),
        out_shape=jax.ShapeDtypeStruct((n, d), BF16),
        scratch_shapes=[pltpu.VMEM((bm, B_OUT_W), BF16)],
        compiler_params=_cparams(("parallel", "arbitrary"), 48),
        name="branch_merge",
    )(ya, *outs, *lses, wa, wb, proj2d, proj2d)


def _out_kernel(h_ref, a_ref, w_ref, o_ref):
    o_ref[...] = h_ref[...] + _dot(a_ref[...], w_ref[...])


def _out_proj(h, a, w, *, bm, bn):
    n, d = h.shape
    k = a.shape[1]
    return pl.pallas_call(
        _out_kernel,
        grid=(n // bm, d // bn),
        in_specs=[pl.BlockSpec((bm, bn), lambda i, j: (i, j)),
                  pl.BlockSpec((bm, k), lambda i, j: (i, 0)),
                  pl.BlockSpec((k, bn), lambda i, j: (0, j))],
        out_specs=pl.BlockSpec((bm, bn), lambda i, j: (i, j)),
        out_shape=jax.ShapeDtypeStruct((n, d), F32),
        compiler_params=_cparams(("parallel", "arbitrary"), 48),
        name="out_proj",
    )(h, a, w)


def _layer(h, g_ffn1, w1_gate, w1_up, w1_down, g_mix, w_in, q_norm_a, k_norm_a,
           w_branch_a, w_branch_b, w_out, g_ffn2, w2_gate, w2_up, w2_down, g_next, *, bsz, last):
    n, d = h.shape
    t = n // bsz
    c = lambda w: w.astype(BF16)
    bm = _tile(n, 512)
    tf = _tile(w1_gate.shape[1], 256)

    h, u = _ffn(h, g_ffn1, _gate_up_blocks(w1_gate, w1_up, tf), c(w1_down), g_mix, emit_hidden=True, bm=bm)
    proj2d = _matmul(u, c(w_in), bm=_tile(n, 1024), bn=_tile(w_in.shape[1], 1024), out_dtype=BF16)
    proj = proj2d.reshape(bsz, t, -1)

    qp, kt = _prep(proj, q_norm_a, k_norm_a, tt=_tile(t, 512))
    ya = _attn_a(qp, kt, proj, tq=_tile(t, 256)).reshape(n, A_Q_W)

    slopes = jnp.exp2(-8.0 * jnp.arange(1, B_HEADS + 1, dtype=F32) / B_HEADS)
    outs, lses = zip(*[_attn_b(proj, slopes, g) for g in range(B_N_GROUPS)])

    merged = _branch(ya, outs, lses, c(w_branch_a), c(w_branch_b), proj2d, bm=bm, bn=_tile(d, 512))
    h = _out_proj(h, merged, c(w_out), bm=_tile(n, 1024), bn=_tile(d, 512))

    wgu2 = _gate_up_blocks(w2_gate, w2_up, tf)
    if last:
        return _ffn(h, g_ffn2, wgu2, c(w2_down), g_next, emit_hidden=False, bm=bm)
    return _ffn(h, g_ffn2, wgu2, c(w2_down), g_next, emit_hidden=True, bm=bm)[0]


def kernel(x, g_ffn1, w1_gate, w1_up, w1_down, g_mix, w_in, q_norm_a, k_norm_a, w_branch_a, w_branch_b,
           w_out, g_ffn2, w2_gate, w2_up, w2_down, g_final):
    bsz, t, d = x.shape
    depth = g_ffn1.shape[0]
    h = x.reshape(bsz * t, d)
    for l in range(depth):
        last = l == depth - 1
        h = _layer(h, g_ffn1[l], w1_gate[l], w1_up[l], w1_down[l], g_mix[l], w_in[l], q_norm_a[l],
                   k_norm_a[l], w_branch_a[l], w_branch_b[l], w_out[l], g_ffn2[l], w2_gate[l], w2_up[l],
                   w2_down[l], g_final if last else g_ffn1[l + 1], bsz=bsz, last=last)
    return h.reshape(bsz, t, d)
```

```python
import functools

import jax
import jax.numpy as jnp
from jax import lax
from jax.experimental import pallas as pl
from jax.experimental.pallas import tpu as pltpu

HEAD_DIM = 128
A_HEADS = 16
A_KV_HEADS = 4
A_GROUP = A_HEADS // A_KV_HEADS
B_PATTERNS = ((128, 1), (512, 4), (2048, 16))
B_HEADS_PER_GROUP = 8
B_N_GROUPS = len(B_PATTERNS)
B_HEADS = B_N_GROUPS * B_HEADS_PER_GROUP
A_Q_W = A_HEADS * HEAD_DIM
A_KV_W = A_KV_HEADS * HEAD_DIM
B_QKV_W = 3 * B_HEADS * HEAD_DIM
B_OUT_W = B_HEADS_PER_GROUP * HEAD_DIM
GRID_W = 64
ROPE_THETA = 10000.0
ROPE_AXIS_DIM = HEAD_DIM // 2
RMS_EPS = 1e-6
NEG_INF = -1e30

DOWN_CHUNK = 512
NORM_ROWS = 16
MIB = 1024 * 1024
BF16 = jnp.bfloat16
F32 = jnp.float32


def _cparams(sem, vmem_mib):
    return pltpu.CompilerParams(dimension_semantics=sem, vmem_limit_bytes=vmem_mib * MIB)


def _tile(n, pref):
    t = min(n, pref)
    while n % t:
        t //= 2
    return t


def _rms(x, gain):
    y = x * lax.rsqrt(jnp.mean(x * x, axis=-1, keepdims=True) + RMS_EPS)
    return y * gain


def _dot(a, b):
    return jnp.dot(a, b, preferred_element_type=F32)


def _norm_rows(src_ref, gain_ref, dst_ref, copy_ref=None):
    def chunk(r, carry):
        rows = pl.ds(pl.multiple_of(r * NORM_ROWS, NORM_ROWS), NORM_ROWS)
        x = src_ref[rows, :]
        dst_ref[rows, :] = _rms(x, gain_ref[...]).astype(dst_ref.dtype)
        if copy_ref is not None:
            copy_ref[rows, :] = x
        return carry

    lax.fori_loop(0, src_ref.shape[0] // NORM_ROWS, chunk, 0, unroll=4)


def _ffn_kernel(x_ref, gin_ref, wg_ref, wu_ref, wd_ref, gout_ref, *rest, emit_hidden):
    if emit_hidden:
        h_ref, y_ref, xn_ref = rest
        acc_ref = h_ref
    else:
        y_ref, xn_ref = rest
        acc_ref = y_ref
    j = pl.program_id(1)

    @pl.when(j == 0)
    def _():
        _norm_rows(x_ref, gin_ref, xn_ref, copy_ref=acc_ref)

    xn = xn_ref[...]
    gate = _dot(xn, wg_ref[...])
    up = _dot(xn, wu_ref[...])
    act = (jax.nn.silu(gate) * up * 0.5).astype(BF16)
    for c in range(0, acc_ref.shape[1], DOWN_CHUNK):
        acc_ref[:, c:c + DOWN_CHUNK] += _dot(act, wd_ref[:, c:c + DOWN_CHUNK])

    @pl.when(j == pl.num_programs(1) - 1)
    def _():
        _norm_rows(acc_ref, gout_ref, y_ref)


def _ffn(x, g_in, wg, wu, wd, g_out, *, emit_hidden, bm, tf):
    n, d = x.shape
    grid = (n // bm, wg.shape[1] // tf)
    row = pl.BlockSpec((bm, d), lambda i, j: (i, 0))
    vec = pl.BlockSpec((1, d), lambda i, j: (0, 0))
    in_specs = [pl.BlockSpec((bm, d), lambda i, j: (i, 0), pipeline_mode=pl.Buffered(1)), vec,
                pl.BlockSpec((d, tf), lambda i, j: (0, j)),
                pl.BlockSpec((d, tf), lambda i, j: (0, j)),
                pl.BlockSpec((tf, d), lambda i, j: (j, 0)),
                vec]
    if emit_hidden:
        out_shape = (jax.ShapeDtypeStruct((n, d), F32), jax.ShapeDtypeStruct((n, d), BF16))
        out_specs = (row, row)
    else:
        out_shape = jax.ShapeDtypeStruct((n, d), F32)
        out_specs = row
    return pl.pallas_call(
        functools.partial(_ffn_kernel, emit_hidden=emit_hidden),
        grid=grid, in_specs=in_specs, out_specs=out_specs, out_shape=out_shape,
        scratch_shapes=[pltpu.VMEM((bm, d), BF16)],
        compiler_params=_cparams(("parallel", "arbitrary"), 58),
        name="ffn_hidden" if emit_hidden else "ffn_final",
    )(x, g_in.reshape(1, d), wg, wu, wd, g_out.reshape(1, d))


def _mm_kernel(a_ref, b_ref, o_ref):
    o_ref[...] = _dot(a_ref[...], b_ref[...]).astype(o_ref.dtype)


def _matmul(a, b, *, bm, bn, out_dtype):
    m, k = a.shape
    n = b.shape[1]
    return pl.pallas_call(
        _mm_kernel,
        grid=(m // bm, n // bn),
        in_specs=[pl.BlockSpec((bm, k), lambda i, j: (i, 0)),
                  pl.BlockSpec((k, bn), lambda i, j: (0, j))],
        out_specs=pl.BlockSpec((bm, bn), lambda i, j: (i, j)),
        out_shape=jax.ShapeDtypeStruct((m, n), out_dtype),
        compiler_params=_cparams(("parallel", "arbitrary"), 48),
        name="in_proj",
    )(a, b)


def _rope(y, cos, sin_signed, first_half):
    partner = jnp.where(first_half, pltpu.roll(y, HEAD_DIM - 32, axis=1), pltpu.roll(y, 32, axis=1))
    return y * cos + partner * sin_signed


def _prep_kernel(q_ref, k_ref, cos_ref, sin_ref, qg_ref, kg_ref, qo_ref, kt_ref):
    cos = cos_ref[...]
    sin = sin_ref[...]
    lane = lax.broadcasted_iota(jnp.int32, cos.shape, 1)
    first_half = (lane % 64) < 32
    scale = HEAD_DIM ** -0.5
    for h in range(A_HEADS):
        sl = slice(h * HEAD_DIM, (h + 1) * HEAD_DIM)
        y = _rms(q_ref[0, :, sl].astype(F32), qg_ref[...])
        qo_ref[0, :, sl] = (_rope(y, cos, sin, first_half) * scale).astype(BF16)
    for h in range(A_KV_HEADS):
        sl = slice(h * HEAD_DIM, (h + 1) * HEAD_DIM)
        y = _rms(k_ref[0, :, sl].astype(F32), kg_ref[...])
        kt_ref[0, sl, :] = _rope(y, cos, sin, first_half).T.astype(BF16)


def _rope_tables(t):
    rows = t // GRID_W
    row_ids = jnp.repeat(jnp.arange(rows), GRID_W).astype(F32)
    col_ids = jnp.tile(jnp.arange(GRID_W), rows).astype(F32)
    inv = ROPE_THETA ** (-jnp.arange(0, ROPE_AXIS_DIM, 2, dtype=F32) / ROPE_AXIS_DIM)
    ang_r = row_ids[:, None] * inv[None, :]
    ang_c = col_ids[:, None] * inv[None, :]
    cos = jnp.concatenate([jnp.cos(ang_r)] * 2 + [jnp.cos(ang_c)] * 2, axis=-1)
    sin = jnp.concatenate([-jnp.sin(ang_r), jnp.sin(ang_r), -jnp.sin(ang_c), jnp.sin(ang_c)], axis=-1)
    return cos, sin


def _prep(proj, q_gain, k_gain, *, tt):
    b, t, _ = proj.shape
    cos, sin = _rope_tables(t)
    tab = pl.BlockSpec((tt, HEAD_DIM), lambda bi, i: (i, 0))
    vec = pl.BlockSpec((1, HEAD_DIM), lambda bi, i: (0, 0))
    return pl.pallas_call(
        _prep_kernel,
        grid=(b, t // tt),
        in_specs=[pl.BlockSpec((1, tt, A_Q_W), lambda bi, i: (bi, i, 0)),
                  pl.BlockSpec((1, tt, A_KV_W), lambda bi, i: (bi, i, A_Q_W // A_KV_W)),
                  tab, tab, vec, vec],
        out_specs=(pl.BlockSpec((1, tt, A_Q_W), lambda bi, i: (bi, i, 0)),
                   pl.BlockSpec((1, A_KV_W, tt), lambda bi, i: (bi, 0, i))),
        out_shape=(jax.ShapeDtypeStruct((b, t, A_Q_W), BF16),
                   jax.ShapeDtypeStruct((b, A_KV_W, t), BF16)),
        compiler_params=_cparams(("parallel", "parallel"), 32),
        name="qk_prep",
    )(proj, proj, cos, sin, q_gain.reshape(1, HEAD_DIM), k_gain.reshape(1, HEAD_DIM))


def _attn_a_kernel(q_ref, kt_ref, v_ref, o_ref):
    kt = kt_ref[0]
    v = v_ref[0]
    for g in range(A_GROUP):
        sl = slice(g * HEAD_DIM, (g + 1) * HEAD_DIM)
        s = _dot(q_ref[0, :, sl], kt)
        p = jnp.exp(s - jnp.max(s, axis=-1, keepdims=True))
        denom = jnp.sum(p, axis=-1, keepdims=True)
        o = _dot(p.astype(BF16), v) / denom
        o_ref[0, :, sl] = o.astype(o_ref.dtype)


def _attn_a(qp, kt, proj, *, tq):
    b, t, _ = qp.shape
    gw = A_GROUP * HEAD_DIM
    v_blk0 = (A_Q_W + A_KV_W) // HEAD_DIM
    return pl.pallas_call(
        _attn_a_kernel,
        grid=(b, A_KV_HEADS, t // tq),
        in_specs=[pl.BlockSpec((1, tq, gw), lambda bi, kv, i: (bi, i, kv)),
                  pl.BlockSpec((1, HEAD_DIM, t), lambda bi, kv, i: (bi, kv, 0)),
                  pl.BlockSpec((1, t, HEAD_DIM), lambda bi, kv, i: (bi, 0, v_blk0 + kv))],
        out_specs=pl.BlockSpec((1, tq, gw), lambda bi, kv, i: (bi, i, kv)),
        out_shape=jax.ShapeDtypeStruct((b, t, A_Q_W), BF16),
        compiler_params=_cparams(("parallel", "parallel", "arbitrary"), 48),
        name="attn_a",
    )(qp, kt, proj)


def _attn_b_kernel(slopes_ref, q_ref, k_ref, v_ref, o_ref, lse_ref, *, group, dilation, half, tq, kw,
                   heads, interleave):
    length = q_ref.shape[1]
    head0 = group * B_HEADS_PER_GROUP + pl.program_id(1) * heads
    scale = HEAD_DIM ** -0.5
    row = lax.broadcasted_iota(jnp.int32, (tq, kw), 0)
    col = lax.broadcasted_iota(jnp.int32, (tq, kw), 1)

    def tile(i, lanes, slope):
        m0 = pl.multiple_of(i * tq, tq)
        ks = pl.multiple_of(jnp.clip(m0 - half, 0, length - kw), half)
        q = q_ref[0, pl.ds(m0, tq), lanes]
        k = k_ref[0, pl.ds(ks, kw), lanes]
        v = v_ref[0, pl.ds(ks, kw), lanes]
        s = lax.dot_general(q, k, (((1,), (1,)), ((), ())), preferred_element_type=F32) * scale
        dist = jnp.abs(col - row + (ks - m0))
        bias = -slope * (dist * dilation).astype(F32)
        s = jnp.where(dist <= half, s + bias, NEG_INF)
        m = jnp.max(s, axis=-1, keepdims=True)
        p = jnp.exp(s - m)
        denom = jnp.sum(p, axis=-1, keepdims=True)
        o = _dot(p.astype(BF16), v) / denom
        o_ref[0, pl.ds(m0, tq), lanes] = o.astype(o_ref.dtype)
        lse_ref[0, pl.ds(m0, tq), lanes] = jnp.broadcast_to(m + jnp.log(denom), (tq, HEAD_DIM))

    for hh in range(heads):
        lanes = slice(hh * HEAD_DIM, (hh + 1) * HEAD_DIM)
        slope = slopes_ref[head0 + hh]

        def body(it, carry, lanes=lanes, slope=slope):
            for u in range(interleave):
                tile(it * interleave + u, lanes, slope)
            return carry

        lax.fori_loop(0, length // (tq * interleave), body, 0)


def _attn_b(proj, slopes, group):
    window, dilation = B_PATTERNS[group]
    b, t, in_w = proj.shape
    length = t // dilation
    half = (window // 2) // dilation
    tq = min(128, length)
    kw = min(length, tq + 2 * half)
    gw = 3 * B_OUT_W
    c0 = A_Q_W + 2 * A_KV_W + group * gw
    if dilation == 1:
        cls, heads, base = proj, 1, c0 // HEAD_DIM
        blocks_per_tok = in_w // HEAD_DIM
    else:
        cls = proj[:, :, c0:c0 + gw].reshape(b, length, dilation * gw)
        heads, base, blocks_per_tok = B_HEADS_PER_GROUP, 0, 3
    steps = B_HEADS_PER_GROUP // heads
    width = heads * HEAD_DIM

    def in_spec(part):
        off = base + part * steps
        return pl.BlockSpec((1, length, width), lambda bi, h, r, s: (bi, 0, r * blocks_per_tok + off + h))

    out_spec = pl.BlockSpec((1, length, width), lambda bi, h, r, s: (bi, 0, r * steps + h))
    n_tiles = length // tq
    o, lse = pl.pallas_call(
        functools.partial(_attn_b_kernel, group=group, dilation=dilation, half=half, tq=tq, kw=kw,
                          heads=heads, interleave=min(4, n_tiles)),
        grid_spec=pltpu.PrefetchScalarGridSpec(
            num_scalar_prefetch=1,
            grid=(b, steps, dilation),
            in_specs=[in_spec(0), in_spec(1), in_spec(2)],
            out_specs=(out_spec, out_spec)),
        out_shape=(jax.ShapeDtypeStruct((b, length, dilation * B_OUT_W), BF16),
                   jax.ShapeDtypeStruct((b, length, dilation * B_OUT_W), F32)),
        compiler_params=_cparams(("parallel", "parallel", "parallel"), 32),
        name=f"attn_b{group}",
    )(slopes, cls, cls, cls)
    return o.reshape(b * t, B_OUT_W), lse.reshape(b * t, B_OUT_W)


def _branch_kernel(ya_ref, o0_ref, o1_ref, o2_ref, l0_ref, l1_ref, l2_ref, wa_ref, wb_ref,
                   ga_ref, gb_ref, out_ref, yb_ref):
    @pl.when(pl.program_id(1) == 0)
    def _():
        l0, l1, l2 = l0_ref[...], l1_ref[...], l2_ref[...]
        m = jnp.maximum(jnp.maximum(l0, l1), l2)
        e0, e1, e2 = jnp.exp(l0 - m), jnp.exp(l1 - m), jnp.exp(l2 - m)
        tot = e0 + e1 + e2
        yb = ((e0 / tot) * o0_ref[...].astype(F32) + (e1 / tot) * o1_ref[...].astype(F32)
              + (e2 / tot) * o2_ref[...].astype(F32))
        yb_ref[...] = yb.astype(BF16)

    a = _dot(ya_ref[...], wa_ref[...])
    bb = _dot(yb_ref[...], wb_ref[...])
    merged = jax.nn.sigmoid(ga_ref[...].astype(F32)) * a + jax.nn.sigmoid(gb_ref[...].astype(F32)) * bb
    out_ref[...] = merged.astype(out_ref.dtype)


def _branch(ya, outs, lses, wa, wb, proj2d, *, bm, bn):
    n, d = ya.shape[0], wa.shape[1]
    gate0 = (A_Q_W + 2 * A_KV_W + B_QKV_W) // bn
    row = lambda w: pl.BlockSpec((bm, w), lambda i, j: (i, 0))
    return pl.pallas_call(
        _branch_kernel,
        grid=(n // bm, d // bn),
        in_specs=[row(A_Q_W)] + [row(B_OUT_W)] * 6
                 + [pl.BlockSpec((A_Q_W, bn), lambda i, j: (0, j)),
                    pl.BlockSpec((B_OUT_W, bn), lambda i, j: (0, j)),
                    pl.BlockSpec((bm, bn), lambda i, j: (i, gate0 + j)),
                    pl.BlockSpec((bm, bn), lambda i, j: (i, gate0 + d // bn + j))],
        out_specs=pl.BlockSpec((bm, bn), lambda i, j: (i, j)),
        out_shape=jax.ShapeDtypeStruct((n, d), BF16),
        scratch_shapes=[pltpu.VMEM((bm, B_OUT_W), BF16)],
        compiler_params=_cparams(("parallel", "arbitrary"), 48),
        name="branch_merge",
    )(ya, *outs, *lses, wa, wb, proj2d, proj2d)


def _out_kernel(h_ref, a_ref, w_ref, o_ref):
    o_ref[...] = h_ref[...] + _dot(a_ref[...], w_ref[...])


def _out_proj(h, a, w, *, bm, bn):
    n, d = h.shape
    k = a.shape[1]
    return pl.pallas_call(
        _out_kernel,
        grid=(n // bm, d // bn),
        in_specs=[pl.BlockSpec((bm, bn), lambda i, j: (i, j)),
                  pl.BlockSpec((bm, k), lambda i, j: (i, 0)),
                  pl.BlockSpec((k, bn), lambda i, j: (0, j))],
        out_specs=pl.BlockSpec((bm, bn), lambda i, j: (i, j)),
        out_shape=jax.ShapeDtypeStruct((n, d), F32),
        compiler_params=_cparams(("parallel", "arbitrary"), 48),
        name="out_proj",
    )(h, a, w)


def _layer(h, g_ffn1, w1_gate, w1_up, w1_down, g_mix, w_in, q_norm_a, k_norm_a,
           w_branch_a, w_branch_b, w_out, g_ffn2, w2_gate, w2_up, w2_down, g_next, *, bsz, last):
    n, d = h.shape
    t = n // bsz
    c = lambda w: w.astype(BF16)
    bm = _tile(n, 512)
    tf = _tile(w1_gate.shape[1], 256)

    h, u = _ffn(h, g_ffn1, c(w1_gate), c(w1_up), c(w1_down), g_mix, emit_hidden=True, bm=bm, tf=tf)
    proj2d = _matmul(u, c(w_in), bm=_tile(n, 1024), bn=_tile(w_in.shape[1], 1024), out_dtype=BF16)
    proj = proj2d.reshape(bsz, t, -1)

    qp, kt = _prep(proj, q_norm_a, k_norm_a, tt=_tile(t, 512))
    ya = _attn_a(qp, kt, proj, tq=_tile(t, 256)).reshape(n, A_Q_W)

    slopes = jnp.exp2(-8.0 * jnp.arange(1, B_HEADS + 1, dtype=F32) / B_HEADS)
    outs, lses = zip(*[_attn_b(proj, slopes, g) for g in range(B_N_GROUPS)])

    merged = _branch(ya, outs, lses, c(w_branch_a), c(w_branch_b), proj2d, bm=bm, bn=_tile(d, 512))
    h = _out_proj(h, merged, c(w_out), bm=_tile(n, 1024), bn=_tile(d, 512))

    if last:
        return _ffn(h, g_ffn2, c(w2_gate), c(w2_up), c(w2_down), g_next, emit_hidden=False, bm=bm, tf=tf)
    return _ffn(h, g_ffn2, c(w2_gate), c(w2_up), c(w2_down), g_next, emit_hidden=True, bm=bm, tf=tf)[0]


def kernel(x, g_ffn1, w1_gate, w1_up, w1_down, g_mix, w_in, q_norm_a, k_norm_a, w_branch_a, w_branch_b,
           w_out, g_ffn2, w2_gate, w2_up, w2_down, g_final):
    bsz, t, d = x.shape
    depth = g_ffn1.shape[0]
    h = x.reshape(bsz * t, d)
    for l in range(depth):
        last = l == depth - 1
        h = _layer(h, g_ffn1[l], w1_gate[l], w1_up[l], w1_down[l], g_mix[l], w_in[l], q_norm_a[l],
                   k_norm_a[l], w_branch_a[l], w_branch_b[l], w_out[l], g_ffn2[l], w2_gate[l], w2_up[l],
                   w2_down[l], g_final if last else g_ffn1[l + 1], bsz=bsz, last=last)
    return h.reshape(bsz, t, d)
```

```python
import functools

import jax
import jax.numpy as jnp
from jax import lax
from jax.experimental import pallas as pl
from jax.experimental.pallas import tpu as pltpu

HEAD_DIM = 128
A_HEADS = 16
A_KV_HEADS = 4
A_GROUP = A_HEADS // A_KV_HEADS
B_PATTERNS = ((128, 1), (512, 4), (2048, 16))
B_HEADS_PER_GROUP = 8
B_N_GROUPS = len(B_PATTERNS)
B_HEADS = B_N_GROUPS * B_HEADS_PER_GROUP
A_Q_W = A_HEADS * HEAD_DIM
A_KV_W = A_KV_HEADS * HEAD_DIM
B_OUT_W = B_HEADS_PER_GROUP * HEAD_DIM
B_GROUP_W = 3 * B_OUT_W
B_QKV_W = B_N_GROUPS * B_GROUP_W
GRID_W = 64
ROPE_THETA = 10000.0
ROPE_AXIS_DIM = HEAD_DIM // 2
RMS_EPS = 1e-6
NEG_INF = -1e30

DOWN_CHUNK = 512
PROJ_BN = 2 * A_KV_W
MIB = 1024 * 1024
BF16 = jnp.bfloat16
F32 = jnp.float32


def _cparams(sem, vmem_mib):
    return pltpu.CompilerParams(dimension_semantics=sem, vmem_limit_bytes=vmem_mib * MIB)


def _tile(n, pref):
    t = min(n, pref)
    while n % t:
        t //= 2
    return t


def _rms(x, gain):
    y = x * lax.rsqrt(jnp.mean(x * x, axis=-1, keepdims=True) + RMS_EPS)
    return y * gain


def _dot(a, b):
    return jnp.dot(a, b, preferred_element_type=F32)


def _ffn_kernel(x_ref, gin_ref, wg_ref, wu_ref, wd_ref, gout_ref, *rest, emit_hidden):
    if emit_hidden:
        h_ref, y_ref, xn_ref = rest
        acc_ref = h_ref
    else:
        y_ref, xn_ref = rest
        acc_ref = y_ref
    j = pl.program_id(1)

    @pl.when(j == 0)
    def _():
        x = x_ref[...]
        xn_ref[...] = _rms(x, gin_ref[...]).astype(BF16)
        acc_ref[...] = x

    xn = xn_ref[...]
    gate = _dot(xn, wg_ref[...])
    up = _dot(xn, wu_ref[...])
    act = (jax.nn.silu(gate) * up * 0.5).astype(BF16)
    for c in range(0, acc_ref.shape[1], DOWN_CHUNK):
        acc_ref[:, c:c + DOWN_CHUNK] += _dot(act, wd_ref[:, c:c + DOWN_CHUNK])

    @pl.when(j == pl.num_programs(1) - 1)
    def _():
        y_ref[...] = _rms(acc_ref[...], gout_ref[...]).astype(y_ref.dtype)


def _ffn(x, g_in, wg, wu, wd, g_out, *, emit_hidden, bm, tf):
    n, d = x.shape
    grid = (n // bm, wg.shape[1] // tf)
    row = pl.BlockSpec((bm, d), lambda i, j: (i, 0))
    vec = pl.BlockSpec((1, d), lambda i, j: (0, 0))
    in_specs = [pl.BlockSpec((bm, d), lambda i, j: (i, 0), pipeline_mode=pl.Buffered(1)), vec,
                pl.BlockSpec((d, tf), lambda i, j: (0, j)),
                pl.BlockSpec((d, tf), lambda i, j: (0, j)),
                pl.BlockSpec((tf, d), lambda i, j: (j, 0)),
                vec]
    if emit_hidden:
        out_shape = (jax.ShapeDtypeStruct((n, d), F32), jax.ShapeDtypeStruct((n, d), BF16))
        out_specs = (row, row)
    else:
        out_shape = jax.ShapeDtypeStruct((n, d), F32)
        out_specs = row
    return pl.pallas_call(
        functools.partial(_ffn_kernel, emit_hidden=emit_hidden),
        grid=grid, in_specs=in_specs, out_specs=out_specs, out_shape=out_shape,
        scratch_shapes=[pltpu.VMEM((bm, d), BF16)],
        compiler_params=_cparams(("parallel", "arbitrary"), 58),
        name="ffn_hidden" if emit_hidden else "ffn_final",
    )(x, g_in.reshape(1, d), wg, wu, wd, g_out.reshape(1, d))


def _rope(y, cos, sin_signed, first_half):
    partner = jnp.where(first_half, pltpu.roll(y, HEAD_DIM - 32, axis=1), pltpu.roll(y, 32, axis=1))
    return y * cos + partner * sin_signed


def _head_pairs(u_ref, w_ref, width):
    u = u_ref[...]
    for c in range(0, width, 2 * HEAD_DIM):
        yield c // HEAD_DIM, _dot(u, w_ref[:, c:c + 2 * HEAD_DIM])


def _proj_q_kernel(u_ref, w_ref, cos_ref, sin_ref, gain_ref, q_ref):
    cos, sin = cos_ref[...], sin_ref[...]
    first_half = (lax.broadcasted_iota(jnp.int32, cos.shape, 1) % 64) < 32
    scale = HEAD_DIM ** -0.5
    for h0, acc in _head_pairs(u_ref, w_ref, q_ref.shape[1]):
        for h in range(2):
            y = _rope(_rms(acc[:, h * HEAD_DIM:(h + 1) * HEAD_DIM], gain_ref[...]), cos, sin, first_half)
            q_ref[:, (h0 + h) * HEAD_DIM:(h0 + h + 1) * HEAD_DIM] = (y * scale).astype(BF16)


def _proj_kv_kernel(u_ref, w_ref, cos_ref, sin_ref, gain_ref, kt_ref, v_ref):
    cos, sin = cos_ref[...], sin_ref[...]
    first_half = (lax.broadcasted_iota(jnp.int32, cos.shape, 1) % 64) < 32
    for h0, acc in _head_pairs(u_ref, w_ref, 2 * A_KV_W):
        for h in range(2):
            part = acc[:, h * HEAD_DIM:(h + 1) * HEAD_DIM]
            if h0 + h < A_KV_HEADS:
                y = _rope(_rms(part, gain_ref[...]), cos, sin, first_half)
                kt_ref[0, (h0 + h) * HEAD_DIM:(h0 + h + 1) * HEAD_DIM, :] = y.T.astype(BF16)
            else:
                hv = h0 + h - A_KV_HEADS
                v_ref[:, hv * HEAD_DIM:(hv + 1) * HEAD_DIM] = part.astype(BF16)


def _rope_tables(t):
    rows = t // GRID_W
    row_ids = jnp.repeat(jnp.arange(rows), GRID_W).astype(F32)
    col_ids = jnp.tile(jnp.arange(GRID_W), rows).astype(F32)
    inv = ROPE_THETA ** (-jnp.arange(0, ROPE_AXIS_DIM, 2, dtype=F32) / ROPE_AXIS_DIM)
    ang_r = row_ids[:, None] * inv[None, :]
    ang_c = col_ids[:, None] * inv[None, :]
    cos = jnp.concatenate([jnp.cos(ang_r)] * 2 + [jnp.cos(ang_c)] * 2, axis=-1)
    sin = jnp.concatenate([-jnp.sin(ang_r), jnp.sin(ang_r), -jnp.sin(ang_c), jnp.sin(ang_c)], axis=-1)
    return cos, sin


def _proj_a(u, w, q_gain, k_gain, *, bsz, bm):
    n, d = u.shape
    t = n // bsz
    assert t % bm == 0 and A_Q_W % PROJ_BN == 0
    tiles_per_seq = t // bm
    cos, sin = _rope_tables(t)
    u_spec = pl.BlockSpec((bm, d), lambda i, j: (i, 0))
    tab = pl.BlockSpec((bm, HEAD_DIM), lambda i, j: (i % tiles_per_seq, 0))
    vec = pl.BlockSpec((1, HEAD_DIM), lambda i, j: (0, 0))
    qp = pl.pallas_call(
        _proj_q_kernel,
        grid=(n // bm, A_Q_W // PROJ_BN),
        in_specs=[u_spec, pl.BlockSpec((d, PROJ_BN), lambda i, j: (0, j)), tab, tab, vec],
        out_specs=pl.BlockSpec((bm, PROJ_BN), lambda i, j: (i, j)),
        out_shape=jax.ShapeDtypeStruct((n, A_Q_W), BF16),
        compiler_params=_cparams(("parallel", "arbitrary"), 48),
        name="proj_q",
    )(u, w[:, :A_Q_W], cos, sin, q_gain.reshape(1, HEAD_DIM))
    kt, v = pl.pallas_call(
        _proj_kv_kernel,
        grid=(n // bm, 1),
        in_specs=[u_spec, pl.BlockSpec((d, 2 * A_KV_W), lambda i, j: (0, 0)), tab, tab, vec],
        out_specs=(pl.BlockSpec((1, A_KV_W, bm), lambda i, j: (i // tiles_per_seq, 0, i % tiles_per_seq)),
                   pl.BlockSpec((bm, A_KV_W), lambda i, j: (i, 0))),
        out_shape=(jax.ShapeDtypeStruct((bsz, A_KV_W, t), BF16), jax.ShapeDtypeStruct((n, A_KV_W), BF16)),
        compiler_params=_cparams(("parallel", "arbitrary"), 48),
        name="proj_kv",
    )(u, w[:, A_Q_W:], cos, sin, k_gain.reshape(1, HEAD_DIM))
    return qp, kt, v


def _proj_cols_kernel(u_ref, w_ref, o_ref, *, sigmoid):
    acc = _dot(u_ref[...], w_ref[...])
    if sigmoid:
        acc = jax.nn.sigmoid(acc)
    o_ref[...] = acc.astype(o_ref.dtype)


def _proj_cols(u, w, *, bm, sigmoid, name):
    n, d = u.shape
    width = w.shape[1]
    bn = _tile(width, PROJ_BN)
    return pl.pallas_call(
        functools.partial(_proj_cols_kernel, sigmoid=sigmoid),
        grid=(n // bm, width // bn),
        in_specs=[pl.BlockSpec((bm, d), lambda i, j: (i, 0)),
                  pl.BlockSpec((d, bn), lambda i, j: (0, j))],
        out_specs=pl.BlockSpec((bm, bn), lambda i, j: (i, j)),
        out_shape=jax.ShapeDtypeStruct((n, width), BF16),
        compiler_params=_cparams(("parallel", "arbitrary"), 48),
        name=name,
    )(u, w)


def _proj_cls_kernel(u_ref, w_ref, o_ref, acc_ref, *, dilation):
    acc = _dot(u_ref[...], w_ref[...])
    rows = o_ref.shape[1]
    for c in range(acc_ref.shape[0]):
        lanes = slice(c * HEAD_DIM, (c + 1) * HEAD_DIM)
        acc_ref[c] = acc[:, lanes]
        for r in range(dilation):
            o_ref[r, :, lanes] = acc_ref[c, pl.ds(r, rows, stride=dilation), :].astype(o_ref.dtype)


def _proj_cls(u, w, *, bm, dilation, name):
    n, d = u.shape
    width = w.shape[1]
    return pl.pallas_call(
        functools.partial(_proj_cls_kernel, dilation=dilation),
        grid=(n // bm, width // PROJ_BN),
        in_specs=[pl.BlockSpec((bm, d), lambda i, j: (i, 0)),
                  pl.BlockSpec((d, PROJ_BN), lambda i, j: (0, j))],
        out_specs=pl.BlockSpec((dilation, bm // dilation, PROJ_BN), lambda i, j: (0, i, j)),
        out_shape=jax.ShapeDtypeStruct((dilation, n // dilation, width), BF16),
        scratch_shapes=[pltpu.VMEM((PROJ_BN // HEAD_DIM, bm, HEAD_DIM), F32)],
        compiler_params=_cparams(("parallel", "arbitrary"), 56),
        name=name,
    )(u, w)


def _attn_a_kernel(q_ref, kt_ref, v_ref, o_ref):
    kt = kt_ref[0]
    v = v_ref[0]
    for g in range(A_GROUP):
        sl = slice(g * HEAD_DIM, (g + 1) * HEAD_DIM)
        s = _dot(q_ref[0, :, sl], kt)
        p = jnp.exp(s - jnp.max(s, axis=-1, keepdims=True))
        denom = jnp.sum(p, axis=-1, keepdims=True)
        o = _dot(p.astype(BF16), v) / denom
        o_ref[0, :, sl] = o.astype(o_ref.dtype)


def _attn_a(qp, kt, v, *, tq):
    b, t, _ = qp.shape
    gw = A_GROUP * HEAD_DIM
    return pl.pallas_call(
        _attn_a_kernel,
        grid=(b, A_KV_HEADS, t // tq),
        in_specs=[pl.BlockSpec((1, tq, gw), lambda bi, kv, i: (bi, i, kv)),
                  pl.BlockSpec((1, HEAD_DIM, t), lambda bi, kv, i: (bi, kv, 0)),
                  pl.BlockSpec((1, t, HEAD_DIM), lambda bi, kv, i: (bi, 0, kv))],
        out_specs=pl.BlockSpec((1, tq, gw), lambda bi, kv, i: (bi, i, kv)),
        out_shape=jax.ShapeDtypeStruct((b, t, A_Q_W), BF16),
        compiler_params=_cparams(("parallel", "parallel", "arbitrary"), 48),
        name="attn_a",
    )(qp, kt, v)


def _attn_b_kernel(slopes_ref, q_ref, k_ref, v_ref, o_ref, lse_ref, *, group, dilation, half, tq, kw,
                   heads, interleave):
    length = q_ref.shape[0]
    head0 = group * B_HEADS_PER_GROUP + pl.program_id(1) * heads
    scale = HEAD_DIM ** -0.5
    row = lax.broadcasted_iota(jnp.int32, (tq, kw), 0)
    col = lax.broadcasted_iota(jnp.int32, (tq, kw), 1)

    def tile(i, lanes, slope):
        m0 = pl.multiple_of(i * tq, tq)
        ks = pl.multiple_of(jnp.clip(m0 - half, 0, length - kw), half)
        q = q_ref[pl.ds(m0, tq), lanes]
        k = k_ref[pl.ds(ks, kw), lanes]
        v = v_ref[pl.ds(ks, kw), lanes]
        s = lax.dot_general(q, k, (((1,), (1,)), ((), ())), preferred_element_type=F32) * scale
        dist = jnp.abs(col - row + (ks - m0))
        bias = -slope * (dist * dilation).astype(F32)
        s = jnp.where(dist <= half, s + bias, NEG_INF)
        m = jnp.max(s, axis=-1, keepdims=True)
        p = jnp.exp(s - m)
        denom = jnp.sum(p, axis=-1, keepdims=True)
        o = _dot(p.astype(BF16), v) / denom
        o_ref[pl.ds(m0, tq), lanes] = o.astype(o_ref.dtype)
        lse_ref[pl.ds(m0, tq), lanes] = jnp.broadcast_to(m + jnp.log(denom), (tq, HEAD_DIM))

    for hh in range(heads):
        lanes = slice(hh * HEAD_DIM, (hh + 1) * HEAD_DIM)
        slope = slopes_ref[head0 + hh]

        def body(it, carry, lanes=lanes, slope=slope):
            for u in range(interleave):
                tile(it * interleave + u, lanes, slope)
            return carry

        lax.fori_loop(0, length // (tq * interleave), body, 0)


def _attn_b(qkv, slopes, group, *, bsz):
    window, dilation = B_PATTERNS[group]
    length = qkv.shape[-2] // (1 if dilation == 1 else bsz)
    half = (window // 2) // dilation
    tq = min(128, length)
    kw = min(length, tq + 2 * half)
    heads = 1 if dilation == 1 else B_HEADS_PER_GROUP
    steps = B_HEADS_PER_GROUP // heads
    width = heads * HEAD_DIM

    if dilation == 1:
        in_spec = lambda part: pl.BlockSpec((None, length, width), lambda bi, h, r, s: (bi, 0, part * steps + h))
        out_spec = pl.BlockSpec((None, length, width), lambda bi, h, r, s: (bi, 0, h))
        out_dims = (bsz, length, B_OUT_W)
    else:
        in_spec = lambda part: pl.BlockSpec((None, length, width), lambda bi, h, r, s: (r, bi, part * steps + h))
        out_spec = pl.BlockSpec((None, length, width), lambda bi, h, r, s: (r, bi, h))
        out_dims = (dilation, bsz * length, B_OUT_W)
    n_tiles = length // tq
    return pl.pallas_call(
        functools.partial(_attn_b_kernel, group=group, dilation=dilation, half=half, tq=tq, kw=kw,
                          heads=heads, interleave=min(4, n_tiles)),
        grid_spec=pltpu.PrefetchScalarGridSpec(
            num_scalar_prefetch=1,
            grid=(bsz, steps, dilation),
            in_specs=[in_spec(0), in_spec(1), in_spec(2)],
            out_specs=(out_spec, out_spec)),
        out_shape=(jax.ShapeDtypeStruct(out_dims, BF16), jax.ShapeDtypeStruct(out_dims, F32)),
        compiler_params=_cparams(("parallel", "parallel", "parallel"), 40),
        name=f"attn_b{group}",
    )(slopes, qkv, qkv, qkv)


def _branch_kernel(ya_ref, o0_ref, l0_ref, o1_ref, l1_ref, o2_ref, l2_ref, wa_ref, wb_ref,
                   ga_ref, gb_ref, out_ref, yb_ref, on1_ref, ln1_ref, on2_ref, ln2_ref):
    @pl.when(pl.program_id(1) == 0)
    def _():
        for o_ref, l_ref, on_ref, ln_ref in ((o1_ref, l1_ref, on1_ref, ln1_ref),
                                             (o2_ref, l2_ref, on2_ref, ln2_ref)):
            dilation, rows = o_ref.shape[0], o_ref.shape[1]
            for c in range(on_ref.shape[0]):
                lanes = slice(c * HEAD_DIM, (c + 1) * HEAD_DIM)
                for r in range(dilation):
                    on_ref[c, pl.ds(r, rows, stride=dilation), :] = o_ref[r, :, lanes].astype(F32)
                    ln_ref[c, pl.ds(r, rows, stride=dilation), :] = l_ref[r, :, lanes]
        for c in range(on1_ref.shape[0]):
            lanes = slice(c * HEAD_DIM, (c + 1) * HEAD_DIM)
            l0, l1, l2 = l0_ref[:, lanes], ln1_ref[c], ln2_ref[c]
            m = jnp.maximum(jnp.maximum(l0, l1), l2)
            e0, e1, e2 = jnp.exp(l0 - m), jnp.exp(l1 - m), jnp.exp(l2 - m)
            tot = e0 + e1 + e2
            yb = (e0 / tot) * o0_ref[:, lanes].astype(F32) + (e1 / tot) * on1_ref[c] + (e2 / tot) * on2_ref[c]
            yb_ref[:, lanes] = yb.astype(BF16)

    a = _dot(ya_ref[...], wa_ref[...])
    bb = _dot(yb_ref[...], wb_ref[...])
    merged = ga_ref[...].astype(F32) * a + gb_ref[...].astype(F32) * bb
    out_ref[...] = merged.astype(out_ref.dtype)


def _branch(ya, o_l, wa, wb, gates, *, bm, bn):
    n, d = ya.shape[0], wa.shape[1]
    row = lambda w: pl.BlockSpec((bm, w), lambda i, j: (i, 0))

    def cls(arr):
        dil = arr.shape[0]
        return pl.BlockSpec((dil, bm // dil, B_OUT_W), lambda i, j: (0, i, 0))

    (o0, l0), (o1, l1), (o2, l2) = o_l
    return pl.pallas_call(
        _branch_kernel,
        grid=(n // bm, d // bn),
        in_specs=[row(A_Q_W), row(B_OUT_W), row(B_OUT_W), cls(o1), cls(l1), cls(o2), cls(l2),
                  pl.BlockSpec((A_Q_W, bn), lambda i, j: (0, j)),
                  pl.BlockSpec((B_OUT_W, bn), lambda i, j: (0, j)),
                  pl.BlockSpec((bm, bn), lambda i, j: (i, j)),
                  pl.BlockSpec((bm, bn), lambda i, j: (i, d // bn + j))],
        out_specs=pl.BlockSpec((bm, bn), lambda i, j: (i, j)),
        out_shape=jax.ShapeDtypeStruct((n, d), BF16),
        scratch_shapes=[pltpu.VMEM((bm, B_OUT_W), BF16)]
                       + [pltpu.VMEM((B_HEADS_PER_GROUP, bm, HEAD_DIM), F32)] * 4,
        compiler_params=_cparams(("parallel", "arbitrary"), 56),
        name="branch_merge",
    )(ya, o0, l0, o1, l1, o2, l2, wa, wb, gates, gates)


def _out_kernel(h_ref, a_ref, w_ref, o_ref):
    o_ref[...] = h_ref[...] + _dot(a_ref[...], w_ref[...])


def _out_proj(h, a, w, *, bm, bn):
    n, d = h.shape
    k = a.shape[1]
    return pl.pallas_call(
        _out_kernel,
        grid=(n // bm, d // bn),
        in_specs=[pl.BlockSpec((bm, bn), lambda i, j: (i, j)),
                  pl.BlockSpec((bm, k), lambda i, j: (i, 0)),
                  pl.BlockSpec((k, bn), lambda i, j: (0, j))],
        out_specs=pl.BlockSpec((bm, bn), lambda i, j: (i, j)),
        out_shape=jax.ShapeDtypeStruct((n, d), F32),
        compiler_params=_cparams(("parallel", "arbitrary"), 48),
        name="out_proj",
    )(h, a, w)


def _layer(h, g_ffn1, w1_gate, w1_up, w1_down, g_mix, w_in, q_norm_a, k_norm_a,
           w_branch_a, w_branch_b, w_out, g_ffn2, w2_gate, w2_up, w2_down, g_next, *, bsz, last):
    n, d = h.shape
    t = n // bsz
    c = lambda w: w.astype(BF16)
    bm = _tile(n, 512)
    tf = _tile(w1_gate.shape[1], 256)

    h, u = _ffn(h, g_ffn1, c(w1_gate), c(w1_up), c(w1_down), g_mix, emit_hidden=True, bm=bm, tf=tf)

    pm = _tile(t, 1024)
    a_w = A_Q_W + 2 * A_KV_W
    b_cols = lambda g: c(w_in[:, a_w + g * B_GROUP_W:a_w + (g + 1) * B_GROUP_W])
    qp, kt, va = _proj_a(u, c(w_in[:, :a_w]), q_norm_a, k_norm_a, bsz=bsz, bm=pm)
    qkv_b = [_proj_cols(u, b_cols(0), bm=pm, sigmoid=False, name="proj_b0").reshape(bsz, t, B_GROUP_W)]
    for g in range(1, B_N_GROUPS):
        qkv_b.append(_proj_cls(u, b_cols(g), bm=pm, dilation=B_PATTERNS[g][1], name=f"proj_b{g}"))
    gates = _proj_cols(u, c(w_in[:, a_w + B_QKV_W:]), bm=pm, sigmoid=True, name="proj_gates")

    ya = _attn_a(qp.reshape(bsz, t, A_Q_W), kt, va.reshape(bsz, t, A_KV_W), tq=_tile(t, 256)).reshape(n, A_Q_W)

    slopes = jnp.exp2(-8.0 * jnp.arange(1, B_HEADS + 1, dtype=F32) / B_HEADS)
    o_l = [_attn_b(qkv_b[g], slopes, g, bsz=bsz) for g in range(B_N_GROUPS)]
    o_l[0] = tuple(a.reshape(n, B_OUT_W) for a in o_l[0])

    merged = _branch(ya, o_l, c(w_branch_a), c(w_branch_b), gates, bm=bm, bn=_tile(d, 1024))
    h = _out_proj(h, merged, c(w_out), bm=_tile(n, 1024), bn=_tile(d, 512))

    if last:
        return _ffn(h, g_ffn2, c(w2_gate), c(w2_up), c(w2_down), g_next, emit_hidden=False, bm=bm, tf=tf)
    return _ffn(h, g_ffn2, c(w2_gate), c(w2_up), c(w2_down), g_next, emit_hidden=True, bm=bm, tf=tf)[0]


def kernel(x, g_ffn1, w1_gate, w1_up, w1_down, g_mix, w_in, q_norm_a, k_norm_a, w_branch_a, w_branch_b,
           w_out, g_ffn2, w2_gate, w2_up, w2_down, g_final):
    bsz, t, d = x.shape
    depth = g_ffn1.shape[0]
    h = x.reshape(bsz * t, d)
    for l in range(depth):
        last = l == depth - 1
        h = _layer(h, g_ffn1[l], w1_gate[l], w1_up[l], w1_down[l], g_mix[l], w_in[l], q_norm_a[l],
                   k_norm_a[l], w_branch_a[l], w_branch_b[l], w_out[l], g_ffn2[l], w2_gate[l], w2_up[l],
                   w2_down[l], g_final if last else g_ffn1[l + 1], bsz=bsz, last=last)
    return h.reshape(bsz, t, d)
```

```python
import functools

import jax
import jax.numpy as jnp
from jax import lax
from jax.experimental import pallas as pl
from jax.experimental.pallas import tpu as pltpu

HEAD_DIM = 128
A_HEADS = 16
A_KV_HEADS = 4
A_GROUP = A_HEADS // A_KV_HEADS
B_PATTERNS = ((128, 1), (512, 4), (2048, 16))
B_HEADS_PER_GROUP = 8
B_N_GROUPS = len(B_PATTERNS)
B_HEADS = B_N_GROUPS * B_HEADS_PER_GROUP
A_Q_W = A_HEADS * HEAD_DIM
A_KV_W = A_KV_HEADS * HEAD_DIM
B_OUT_W = B_HEADS_PER_GROUP * HEAD_DIM
B_GROUP_W = 3 * B_OUT_W
B_QKV_W = B_N_GROUPS * B_GROUP_W
GRID_W = 64
ROPE_THETA = 10000.0
ROPE_AXIS_DIM = HEAD_DIM // 2
RMS_EPS = 1e-6
NEG_INF = -1e30

DOWN_CHUNK = 512
ATTN_A_ROWS = 256
PROJ_BN = 2 * A_KV_W
MIB = 1024 * 1024
BF16 = jnp.bfloat16
F32 = jnp.float32


def _cparams(sem, vmem_mib):
    return pltpu.CompilerParams(dimension_semantics=sem, vmem_limit_bytes=vmem_mib * MIB)


def _tile(n, pref):
    t = min(n, pref)
    while n % t:
        t //= 2
    return t


def _rms(x, gain):
    y = x * lax.rsqrt(jnp.mean(x * x, axis=-1, keepdims=True) + RMS_EPS)
    return y * gain


def _dot(a, b):
    return jnp.dot(a, b, preferred_element_type=F32)


def _ffn_kernel(x_ref, gin_ref, wg_ref, wu_ref, wd_ref, gout_ref, *rest, emit_hidden, n_cast):
    cast_in, rest = rest[:n_cast], rest[n_cast:]
    if emit_hidden:
        h_ref, y_ref = rest[:2]
        acc_ref = h_ref
    else:
        y_ref = rest[0]
        acc_ref = y_ref
    cast_out, xn_ref = rest[-1 - n_cast:-1], rest[-1]
    j = pl.program_id(1)

    for src_ref, dst_ref in zip(cast_in, cast_out):
        dst_ref[...] = src_ref[...].astype(dst_ref.dtype)

    @pl.when(j == 0)
    def _():
        x = x_ref[...]
        xn_ref[...] = _rms(x, gin_ref[...]).astype(BF16)
        acc_ref[...] = x

    xn = xn_ref[...]
    gate = _dot(xn, wg_ref[...])
    up = _dot(xn, wu_ref[...])
    act = (jax.nn.silu(gate) * up * 0.5).astype(BF16)
    for c in range(0, acc_ref.shape[1], DOWN_CHUNK):
        acc_ref[:, c:c + DOWN_CHUNK] += _dot(act, wd_ref[:, c:c + DOWN_CHUNK])

    @pl.when(j == pl.num_programs(1) - 1)
    def _():
        y_ref[...] = _rms(acc_ref[...], gout_ref[...]).astype(y_ref.dtype)


def _can_cast_along(n, d, f, bm, tf):
    row_tiles = n // bm
    return d % row_tiles == 0 and (d // row_tiles) % HEAD_DIM == 0 and f % tf == 0


def _ffn(x, g_in, wg, wu, wd, g_out, *, emit_hidden, bm, tf, cast_along=()):
    n, d = x.shape
    f = wg.shape[1]
    grid = (n // bm, f // tf)
    row = pl.BlockSpec((bm, d), lambda i, j: (i, 0))
    vec = pl.BlockSpec((1, d), lambda i, j: (0, 0))
    in_specs = [pl.BlockSpec((bm, d), lambda i, j: (i, 0), pipeline_mode=pl.Buffered(1)), vec,
                pl.BlockSpec((d, tf), lambda i, j: (0, j)),
                pl.BlockSpec((d, tf), lambda i, j: (0, j)),
                pl.BlockSpec((tf, d), lambda i, j: (j, 0)),
                vec]
    if emit_hidden:
        out_shape = [jax.ShapeDtypeStruct((n, d), F32), jax.ShapeDtypeStruct((n, d), BF16)]
        out_specs = [row, row]
    else:
        out_shape = [jax.ShapeDtypeStruct((n, d), F32)]
        out_specs = [row]
    dr = d // grid[0]
    for w in cast_along:
        assert w.shape in ((d, f), (f, d)) and _can_cast_along(n, d, f, bm, tf)
        spec = (pl.BlockSpec((dr, tf), lambda i, j: (i, j)) if w.shape == (d, f)
                else pl.BlockSpec((tf, dr), lambda i, j: (j, i)))
        in_specs.append(spec)
        out_specs.append(spec)
        out_shape.append(jax.ShapeDtypeStruct(w.shape, BF16))
    outs = pl.pallas_call(
        functools.partial(_ffn_kernel, emit_hidden=emit_hidden, n_cast=len(cast_along)),
        grid=grid, in_specs=in_specs, out_specs=out_specs, out_shape=out_shape,
        scratch_shapes=[pltpu.VMEM((bm, d), BF16)],
        compiler_params=_cparams(("parallel", "arbitrary"), 60),
        name="ffn_hidden" if emit_hidden else "ffn_final",
    )(x, g_in.reshape(1, d), wg, wu, wd, g_out.reshape(1, d), *cast_along)
    return outs if len(outs) > 1 else outs[0]


def _rope(y, cos, sin_signed, first_half):
    partner = jnp.where(first_half, pltpu.roll(y, HEAD_DIM - 32, axis=1), pltpu.roll(y, 32, axis=1))
    return y * cos + partner * sin_signed


def _head_pairs(u_ref, w_ref, width):
    u = u_ref[...]
    for c in range(0, width, 2 * HEAD_DIM):
        yield c // HEAD_DIM, _dot(u, w_ref[:, c:c + 2 * HEAD_DIM])


def _proj_q_kernel(u_ref, w_ref, cos_ref, sin_ref, gain_ref, q_ref):
    cos, sin = cos_ref[...], sin_ref[...]
    first_half = (lax.broadcasted_iota(jnp.int32, cos.shape, 1) % 64) < 32
    scale = HEAD_DIM ** -0.5
    for h0, acc in _head_pairs(u_ref, w_ref, q_ref.shape[1]):
        for h in range(2):
            y = _rope(_rms(acc[:, h * HEAD_DIM:(h + 1) * HEAD_DIM], gain_ref[...]), cos, sin, first_half)
            q_ref[:, (h0 + h) * HEAD_DIM:(h0 + h + 1) * HEAD_DIM] = (y * scale).astype(BF16)


def _proj_kv_kernel(u_ref, w_ref, cos_ref, sin_ref, gain_ref, kt_ref, v_ref):
    cos, sin = cos_ref[...], sin_ref[...]
    first_half = (lax.broadcasted_iota(jnp.int32, cos.shape, 1) % 64) < 32
    for h0, acc in _head_pairs(u_ref, w_ref, 2 * A_KV_W):
        for h in range(2):
            part = acc[:, h * HEAD_DIM:(h + 1) * HEAD_DIM]
            if h0 + h < A_KV_HEADS:
                y = _rope(_rms(part, gain_ref[...]), cos, sin, first_half)
                kt_ref[0, (h0 + h) * HEAD_DIM:(h0 + h + 1) * HEAD_DIM, :] = y.T.astype(BF16)
            else:
                hv = h0 + h - A_KV_HEADS
                v_ref[:, hv * HEAD_DIM:(hv + 1) * HEAD_DIM] = part.astype(BF16)


def _rope_tables(t):
    rows = t // GRID_W
    row_ids = jnp.repeat(jnp.arange(rows), GRID_W).astype(F32)
    col_ids = jnp.tile(jnp.arange(GRID_W), rows).astype(F32)
    inv = ROPE_THETA ** (-jnp.arange(0, ROPE_AXIS_DIM, 2, dtype=F32) / ROPE_AXIS_DIM)
    ang_r = row_ids[:, None] * inv[None, :]
    ang_c = col_ids[:, None] * inv[None, :]
    cos = jnp.concatenate([jnp.cos(ang_r)] * 2 + [jnp.cos(ang_c)] * 2, axis=-1)
    sin = jnp.concatenate([-jnp.sin(ang_r), jnp.sin(ang_r), -jnp.sin(ang_c), jnp.sin(ang_c)], axis=-1)
    return cos, sin


def _proj_a(u, w, q_gain, k_gain, *, bsz, bm):
    n, d = u.shape
    t = n // bsz
    assert t % bm == 0 and A_Q_W % PROJ_BN == 0
    tiles_per_seq = t // bm
    cos, sin = _rope_tables(t)
    u_spec = pl.BlockSpec((bm, d), lambda i, j: (i, 0))
    tab = pl.BlockSpec((bm, HEAD_DIM), lambda i, j: (i % tiles_per_seq, 0))
    vec = pl.BlockSpec((1, HEAD_DIM), lambda i, j: (0, 0))
    qp = pl.pallas_call(
        _proj_q_kernel,
        grid=(n // bm, A_Q_W // PROJ_BN),
        in_specs=[u_spec, pl.BlockSpec((d, PROJ_BN), lambda i, j: (0, j)), tab, tab, vec],
        out_specs=pl.BlockSpec((bm, PROJ_BN), lambda i, j: (i, j)),
        out_shape=jax.ShapeDtypeStruct((n, A_Q_W), BF16),
        compiler_params=_cparams(("parallel", "arbitrary"), 48),
        name="proj_q",
    )(u, w[:, :A_Q_W], cos, sin, q_gain.reshape(1, HEAD_DIM))
    kt, v = pl.pallas_call(
        _proj_kv_kernel,
        grid=(n // bm, 1),
        in_specs=[u_spec, pl.BlockSpec((d, 2 * A_KV_W), lambda i, j: (0, 0)), tab, tab, vec],
        out_specs=(pl.BlockSpec((1, A_KV_W, bm), lambda i, j: (i // tiles_per_seq, 0, i % tiles_per_seq)),
                   pl.BlockSpec((bm, A_KV_W), lambda i, j: (i, 0))),
        out_shape=(jax.ShapeDtypeStruct((bsz, A_KV_W, t), BF16), jax.ShapeDtypeStruct((n, A_KV_W), BF16)),
        compiler_params=_cparams(("parallel", "arbitrary"), 48),
        name="proj_kv",
    )(u, w[:, A_Q_W:], cos, sin, k_gain.reshape(1, HEAD_DIM))
    return qp, kt, v


def _proj_cols_kernel(u_ref, w_ref, o_ref, *, sigmoid):
    acc = _dot(u_ref[...], w_ref[...])
    if sigmoid:
        acc = jax.nn.sigmoid(acc)
    o_ref[...] = acc.astype(o_ref.dtype)


def _proj_cols(u, w, *, bm, sigmoid, name):
    n, d = u.shape
    width = w.shape[1]
    bn = _tile(width, PROJ_BN)
    return pl.pallas_call(
        functools.partial(_proj_cols_kernel, sigmoid=sigmoid),
        grid=(n // bm, width // bn),
        in_specs=[pl.BlockSpec((bm, d), lambda i, j: (i, 0)),
                  pl.BlockSpec((d, bn), lambda i, j: (0, j))],
        out_specs=pl.BlockSpec((bm, bn), lambda i, j: (i, j)),
        out_shape=jax.ShapeDtypeStruct((n, width), BF16),
        compiler_params=_cparams(("parallel", "arbitrary"), 48),
        name=name,
    )(u, w)


def _proj_cls_kernel(u_ref, w_ref, o_ref, acc_ref, *, dilation):
    acc = _dot(u_ref[...], w_ref[...])
    rows = o_ref.shape[1]
    for c in range(acc_ref.shape[0]):
        lanes = slice(c * HEAD_DIM, (c + 1) * HEAD_DIM)
        acc_ref[c] = acc[:, lanes]
        for r in range(dilation):
            o_ref[r, :, lanes] = acc_ref[c, pl.ds(r, rows, stride=dilation), :].astype(o_ref.dtype)


def _proj_cls(u, w, *, bm, dilation, name):
    n, d = u.shape
    width = w.shape[1]
    return pl.pallas_call(
        functools.partial(_proj_cls_kernel, dilation=dilation),
        grid=(n // bm, width // PROJ_BN),
        in_specs=[pl.BlockSpec((bm, d), lambda i, j: (i, 0)),
                  pl.BlockSpec((d, PROJ_BN), lambda i, j: (0, j))],
        out_specs=pl.BlockSpec((dilation, bm // dilation, PROJ_BN), lambda i, j: (0, i, j)),
        out_shape=jax.ShapeDtypeStruct((dilation, n // dilation, width), BF16),
        scratch_shapes=[pltpu.VMEM((PROJ_BN // HEAD_DIM, bm, HEAD_DIM), F32)],
        compiler_params=_cparams(("parallel", "arbitrary"), 56),
        name=name,
    )(u, w)


def _attn_a_kernel(q_ref, kt_ref, v_ref, o_ref):
    kt = kt_ref[0]
    v = v_ref[0]
    for r0 in range(0, q_ref.shape[1], ATTN_A_ROWS):
        rows = slice(r0, r0 + ATTN_A_ROWS)
        for g in range(A_GROUP):
            sl = slice(g * HEAD_DIM, (g + 1) * HEAD_DIM)
            s = _dot(q_ref[0, rows, sl], kt)
            p = jnp.exp(s - jnp.max(s, axis=-1, keepdims=True))
            denom = jnp.sum(p, axis=-1, keepdims=True)
            o = _dot(p.astype(BF16), v) / denom
            o_ref[0, rows, sl] = o.astype(o_ref.dtype)


def _attn_a(qp, kt, v, *, tq):
    b, t, _ = qp.shape
    gw = A_GROUP * HEAD_DIM
    assert tq % ATTN_A_ROWS == 0
    return pl.pallas_call(
        _attn_a_kernel,
        grid=(b, A_KV_HEADS, t // tq),
        in_specs=[pl.BlockSpec((1, tq, gw), lambda bi, kv, i: (bi, i, kv)),
                  pl.BlockSpec((1, HEAD_DIM, t), lambda bi, kv, i: (bi, kv, 0)),
                  pl.BlockSpec((1, t, HEAD_DIM), lambda bi, kv, i: (bi, 0, kv))],
        out_specs=pl.BlockSpec((1, tq, gw), lambda bi, kv, i: (bi, i, kv)),
        out_shape=jax.ShapeDtypeStruct((b, t, A_Q_W), BF16),
        compiler_params=_cparams(("parallel", "parallel", "arbitrary"), 48),
        name="attn_a",
    )(qp, kt, v)


def _attn_b_kernel(slopes_ref, q_ref, k_ref, v_ref, o_ref, lse_ref, *, group, dilation, half, tq, kw,
                   heads, interleave):
    length = q_ref.shape[0]
    head0 = group * B_HEADS_PER_GROUP + pl.program_id(1) * heads
    scale = HEAD_DIM ** -0.5
    row = lax.broadcasted_iota(jnp.int32, (tq, kw), 0)
    col = lax.broadcasted_iota(jnp.int32, (tq, kw), 1)

    def tile(i, lanes, slope):
        m0 = pl.multiple_of(i * tq, tq)
        ks = pl.multiple_of(jnp.clip(m0 - half, 0, length - kw), half)
        q = q_ref[pl.ds(m0, tq), lanes]
        k = k_ref[pl.ds(ks, kw), lanes]
        v = v_ref[pl.ds(ks, kw), lanes]
        s = lax.dot_general(q, k, (((1,), (1,)), ((), ())), preferred_element_type=F32) * scale
        dist = jnp.abs(col - row + (ks - m0))
        bias = -slope * (dist * dilation).astype(F32)
        s = jnp.where(dist <= half, s + bias, NEG_INF)
        m = jnp.max(s, axis=-1, keepdims=True)
        p = jnp.exp(s - m)
        denom = jnp.sum(p, axis=-1, keepdims=True)
        o = _dot(p.astype(BF16), v) / denom
        o_ref[pl.ds(m0, tq), lanes] = o.astype(o_ref.dtype)
        lse_ref[pl.ds(m0, tq), lanes] = jnp.broadcast_to(m + jnp.log(denom), (tq, HEAD_DIM))

    for hh in range(heads):
        lanes = slice(hh * HEAD_DIM, (hh + 1) * HEAD_DIM)
        slope = slopes_ref[head0 + hh]

        def body(it, carry, lanes=lanes, slope=slope):
            for u in range(interleave):
                tile(it * interleave + u, lanes, slope)
            return carry

        lax.fori_loop(0, length // (tq * interleave), body, 0)


def _attn_b(qkv, slopes, group, *, bsz):
    window, dilation = B_PATTERNS[group]
    length = qkv.shape[-2] // (1 if dilation == 1 else bsz)
    half = (window // 2) // dilation
    tq = min(128, length)
    kw = min(length, tq + 2 * half)
    heads = 1 if dilation == 1 else B_HEADS_PER_GROUP
    steps = B_HEADS_PER_GROUP // heads
    width = heads * HEAD_DIM

    if dilation == 1:
        in_spec = lambda part: pl.BlockSpec((None, length, width), lambda bi, h, r, s: (bi, 0, part * steps + h))
        out_spec = pl.BlockSpec((None, length, width), lambda bi, h, r, s: (bi, 0, h))
        out_dims = (bsz, length, B_OUT_W)
    else:
        in_spec = lambda part: pl.BlockSpec((None, length, width), lambda bi, h, r, s: (r, bi, part * steps + h))
        out_spec = pl.BlockSpec((None, length, width), lambda bi, h, r, s: (r, bi, h))
        out_dims = (dilation, bsz * length, B_OUT_W)
    n_tiles = length // tq
    return pl.pallas_call(
        functools.partial(_attn_b_kernel, group=group, dilation=dilation, half=half, tq=tq, kw=kw,
                          heads=heads, interleave=min(4, n_tiles)),
        grid_spec=pltpu.PrefetchScalarGridSpec(
            num_scalar_prefetch=1,
            grid=(bsz, steps, dilation),
            in_specs=[in_spec(0), in_spec(1), in_spec(2)],
            out_specs=(out_spec, out_spec)),
        out_shape=(jax.ShapeDtypeStruct(out_dims, BF16), jax.ShapeDtypeStruct(out_dims, F32)),
        compiler_params=_cparams(("parallel", "parallel", "parallel"), 40),
        name=f"attn_b{group}",
    )(slopes, qkv, qkv, qkv)


def _branch_kernel(ya_ref, o0_ref, l0_ref, o1_ref, l1_ref, o2_ref, l2_ref, wa_ref, wb_ref,
                   ga_ref, gb_ref, out_ref, yb_ref, on1_ref, ln1_ref, on2_ref, ln2_ref):
    @pl.when(pl.program_id(1) == 0)
    def _():
        for o_ref, l_ref, on_ref, ln_ref in ((o1_ref, l1_ref, on1_ref, ln1_ref),
                                             (o2_ref, l2_ref, on2_ref, ln2_ref)):
            dilation, rows = o_ref.shape[0], o_ref.shape[1]
            for c in range(on_ref.shape[0]):
                lanes = slice(c * HEAD_DIM, (c + 1) * HEAD_DIM)
                for r in range(dilation):
                    on_ref[c, pl.ds(r, rows, stride=dilation), :] = o_ref[r, :, lanes].astype(F32)
                    ln_ref[c, pl.ds(r, rows, stride=dilation), :] = l_ref[r, :, lanes]
        for c in range(on1_ref.shape[0]):
            lanes = slice(c * HEAD_DIM, (c + 1) * HEAD_DIM)
            l0, l1, l2 = l0_ref[:, lanes], ln1_ref[c], ln2_ref[c]
            m = jnp.maximum(jnp.maximum(l0, l1), l2)
            e0, e1, e2 = jnp.exp(l0 - m), jnp.exp(l1 - m), jnp.exp(l2 - m)
            tot = e0 + e1 + e2
            yb = (e0 / tot) * o0_ref[:, lanes].astype(F32) + (e1 / tot) * on1_ref[c] + (e2 / tot) * on2_ref[c]
            yb_ref[:, lanes] = yb.astype(BF16)

    a = _dot(ya_ref[...], wa_ref[...])
    bb = _dot(yb_ref[...], wb_ref[...])
    merged = ga_ref[...].astype(F32) * a + gb_ref[...].astype(F32) * bb
    out_ref[...] = merged.astype(out_ref.dtype)


def _branch(ya, o_l, wa, wb, gates, *, bm, bn):
    n, d = ya.shape[0], wa.shape[1]
    row = lambda w: pl.BlockSpec((bm, w), lambda i, j: (i, 0))

    def cls(arr):
        dil = arr.shape[0]
        return pl.BlockSpec((dil, bm // dil, B_OUT_W), lambda i, j: (0, i, 0))

    (o0, l0), (o1, l1), (o2, l2) = o_l
    return pl.pallas_call(
        _branch_kernel,
        grid=(n // bm, d // bn),
        in_specs=[row(A_Q_W), row(B_OUT_W), row(B_OUT_W), cls(o1), cls(l1), cls(o2), cls(l2),
                  pl.BlockSpec((A_Q_W, bn), lambda i, j: (0, j)),
                  pl.BlockSpec((B_OUT_W, bn), lambda i, j: (0, j)),
                  pl.BlockSpec((bm, bn), lambda i, j: (i, j)),
                  pl.BlockSpec((bm, bn), lambda i, j: (i, d // bn + j))],
        out_specs=pl.BlockSpec((bm, bn), lambda i, j: (i, j)),
        out_shape=jax.ShapeDtypeStruct((n, d), BF16),
        scratch_shapes=[pltpu.VMEM((bm, B_OUT_W), BF16)]
                       + [pltpu.VMEM((B_HEADS_PER_GROUP, bm, HEAD_DIM), F32)] * 4,
        compiler_params=_cparams(("parallel", "arbitrary"), 56),
        name="branch_merge",
    )(ya, o0, l0, o1, l1, o2, l2, wa, wb, gates, gates)


def _out_kernel(h_ref, a_ref, w_ref, o_ref):
    o_ref[...] = h_ref[...] + _dot(a_ref[...], w_ref[...])


def _out_proj(h, a, w, *, bm, bn):
    n, d = h.shape
    k = a.shape[1]
    return pl.pallas_call(
        _out_kernel,
        grid=(n // bm, d // bn),
        in_specs=[pl.BlockSpec((bm, bn), lambda i, j: (i, j)),
                  pl.BlockSpec((bm, k), lambda i, j: (i, 0)),
                  pl.BlockSpec((k, bn), lambda i, j: (0, j))],
        out_specs=pl.BlockSpec((bm, bn), lambda i, j: (i, j)),
        out_shape=jax.ShapeDtypeStruct((n, d), F32),
        compiler_params=_cparams(("parallel", "arbitrary"), 48),
        name="out_proj",
    )(h, a, w)


def _layer(h, g_ffn1, w1_gate, w1_up, w1_down, g_mix, w_in, q_norm_a, k_norm_a,
           w_branch_a, w_branch_b, w_out, g_ffn2, w2_gate, w2_up, w2_down, g_next, *, bsz, last):
    n, d = h.shape
    t = n // bsz
    c = lambda w: w.astype(BF16)
    bm = _tile(n, 512)
    tf = _tile(w1_gate.shape[1], 256)

    ffn2_w = (w2_gate, w2_up, w2_down)
    if w2_gate.shape == w1_gate.shape and _can_cast_along(n, d, w1_gate.shape[1], bm, tf):
        h, u, *ffn2_w = _ffn(h, g_ffn1, c(w1_gate), c(w1_up), c(w1_down), g_mix, emit_hidden=True, bm=bm,
                             tf=tf, cast_along=ffn2_w)
    else:
        h, u = _ffn(h, g_ffn1, c(w1_gate), c(w1_up), c(w1_down), g_mix, emit_hidden=True, bm=bm, tf=tf)
        ffn2_w = [c(w) for w in ffn2_w]

    pm = _tile(t, 1024)
    a_w = A_Q_W + 2 * A_KV_W
    b_cols = lambda g: c(w_in[:, a_w + g * B_GROUP_W:a_w + (g + 1) * B_GROUP_W])
    qp, kt, va = _proj_a(u, c(w_in[:, :a_w]), q_norm_a, k_norm_a, bsz=bsz, bm=pm)
    qkv_b = [_proj_cols(u, b_cols(0), bm=pm, sigmoid=False, name="proj_b0").reshape(bsz, t, B_GROUP_W)]
    for g in range(1, B_N_GROUPS):
        qkv_b.append(_proj_cls(u, b_cols(g), bm=pm, dilation=B_PATTERNS[g][1], name=f"proj_b{g}"))
    gates = _proj_cols(u, c(w_in[:, a_w + B_QKV_W:]), bm=pm, sigmoid=True, name="proj_gates")

    ya = _attn_a(qp.reshape(bsz, t, A_Q_W), kt, va.reshape(bsz, t, A_KV_W), tq=_tile(t, 1024)).reshape(n, A_Q_W)

    slopes = jnp.exp2(-8.0 * jnp.arange(1, B_HEADS + 1, dtype=F32) / B_HEADS)
    o_l = [_attn_b(qkv_b[g], slopes, g, bsz=bsz) for g in range(B_N_GROUPS)]
    o_l[0] = tuple(a.reshape(n, B_OUT_W) for a in o_l[0])

    merged = _branch(ya, o_l, c(w_branch_a), c(w_branch_b), gates, bm=bm, bn=_tile(d, 1024))
    h = _out_proj(h, merged, c(w_out), bm=_tile(n, 1024), bn=_tile(d, 512))

    if last:
        return _ffn(h, g_ffn2, *ffn2_w, g_next, emit_hidden=False, bm=bm, tf=tf)
    return _ffn(h, g_ffn2, *ffn2_w, g_next, emit_hidden=True, bm=bm, tf=tf)[0]


def kernel(x, g_ffn1, w1_gate, w1_up, w1_down, g_mix, w_in, q_norm_a, k_norm_a, w_branch_a, w_branch_b,
           w_out, g_ffn2, w2_gate, w2_up, w2_down, g_final):
    bsz, t, d = x.shape
    depth = g_ffn1.shape[0]
    h = x.reshape(bsz * t, d)
    for l in range(depth):
        last = l == depth - 1
        h = _layer(h, g_ffn1[l], w1_gate[l], w1_up[l], w1_down[l], g_mix[l], w_in[l], q_norm_a[l],
                   k_norm_a[l], w_branch_a[l], w_branch_b[l], w_out[l], g_ffn2[l], w2_gate[l], w2_up[l],
                   w2_down[l], g_final if last else g_ffn1[l + 1], bsz=bsz, last=last)
    return h.reshape(bsz, t, d)
```

```python
import functools

import jax
import jax.numpy as jnp
from jax import lax
from jax.experimental import pallas as pl
from jax.experimental.pallas import tpu as pltpu

HEAD_DIM = 128
A_HEADS = 16
A_KV_HEADS = 4
A_GROUP = A_HEADS // A_KV_HEADS
B_PATTERNS = ((128, 1), (512, 4), (2048, 16))
B_HEADS_PER_GROUP = 8
B_N_GROUPS = len(B_PATTERNS)
B_HEADS = B_N_GROUPS * B_HEADS_PER_GROUP
A_Q_W = A_HEADS * HEAD_DIM
A_KV_W = A_KV_HEADS * HEAD_DIM
B_OUT_W = B_HEADS_PER_GROUP * HEAD_DIM
B_GROUP_W = 3 * B_OUT_W
B_QKV_W = B_N_GROUPS * B_GROUP_W
GRID_W = 64
ROPE_THETA = 10000.0
ROPE_AXIS_DIM = HEAD_DIM // 2
RMS_EPS = 1e-6
NEG_INF = -1e30

DOWN_CHUNK = 512
FFN_MM_ROWS = 512
ATTN_A_ROWS = 256
PROJ_BN = 2 * A_KV_W
MIB = 1024 * 1024
BF16 = jnp.bfloat16
F32 = jnp.float32


def _cparams(sem, vmem_mib):
    return pltpu.CompilerParams(dimension_semantics=sem, vmem_limit_bytes=vmem_mib * MIB)


def _tile(n, pref):
    t = min(n, pref)
    while n % t:
        t //= 2
    return t


def _rms(x, gain):
    y = x * lax.rsqrt(jnp.mean(x * x, axis=-1, keepdims=True) + RMS_EPS)
    return y * gain


def _dot(a, b):
    return jnp.dot(a, b, preferred_element_type=F32)


def _ffn_kernel(x_ref, gin_ref, wg_ref, wu_ref, wd_ref, gout_ref, *rest, emit_hidden, n_cast, n_io, n_f):
    cast_in, rest = rest[:n_cast], rest[n_cast:]
    outs, rest = rest[:2 if emit_hidden else 1], rest[2 if emit_hidden else 1:]
    cast_out, (xn_ref, acc_ref) = rest[:n_cast], rest[n_cast:]
    y_ref = outs[-1]
    io_rows = x_ref.shape[0]
    j = pl.program_id(1)
    first_mm, last_mm = n_io - 1, n_io + n_f - 2

    for src_ref, dst_ref in zip(cast_in, cast_out):
        dst_ref[...] = src_ref[...].astype(dst_ref.dtype)

    @pl.when(j <= first_mm)
    def _():
        rows = pl.ds(pl.multiple_of(j * io_rows, io_rows), io_rows)
        x = x_ref[...]
        xn_ref[rows, :] = _rms(x, gin_ref[...]).astype(BF16)
        acc_ref[rows, :] = x

    @pl.when((j >= first_mm) & (j <= last_mm))
    def _():
        for r0 in range(0, acc_ref.shape[0], FFN_MM_ROWS):
            rows = slice(r0, r0 + FFN_MM_ROWS)
            xn = xn_ref[rows, :]
            gate = _dot(xn, wg_ref[...])
            up = _dot(xn, wu_ref[...])
            act = (jax.nn.silu(gate) * up * 0.5).astype(BF16)
            for c in range(0, acc_ref.shape[1], DOWN_CHUNK):
                acc_ref[rows, c:c + DOWN_CHUNK] += _dot(act, wd_ref[:, c:c + DOWN_CHUNK])

    @pl.when(j >= last_mm)
    def _():
        rows = pl.ds(pl.multiple_of((j - last_mm) * io_rows, io_rows), io_rows)
        h = acc_ref[rows, :]
        if emit_hidden:
            outs[0][...] = h
        y_ref[...] = _rms(h, gout_ref[...]).astype(y_ref.dtype)


def _can_cast_along(n, d, f, bm, tf):
    row_tiles = n // bm
    return d % row_tiles == 0 and (d // row_tiles) % HEAD_DIM == 0 and f % tf == 0


def _ffn(x, g_in, wg, wu, wd, g_out, *, emit_hidden, bm, tf, io_rows, cast_along=()):
    n, d = x.shape
    f = wg.shape[1]
    n_io, n_f = bm // io_rows, f // tf
    assert bm % FFN_MM_ROWS == 0 and bm % io_rows == 0
    grid = (n // bm, n_f + 2 * (n_io - 1))
    fblk = lambda j: jnp.clip(j - (n_io - 1), 0, n_f - 1)
    vec = pl.BlockSpec((1, d), lambda i, j: (0, 0))
    in_specs = [pl.BlockSpec((io_rows, d), lambda i, j: (i * n_io + jnp.minimum(j, n_io - 1), 0)), vec,
                pl.BlockSpec((d, tf), lambda i, j: (0, fblk(j))),
                pl.BlockSpec((d, tf), lambda i, j: (0, fblk(j))),
                pl.BlockSpec((tf, d), lambda i, j: (fblk(j), 0)),
                vec]
    out_row = pl.BlockSpec((io_rows, d), lambda i, j: (i * n_io + jnp.clip(j - (n_io + n_f - 2), 0, n_io - 1), 0))
    if emit_hidden:
        out_shape = [jax.ShapeDtypeStruct((n, d), F32), jax.ShapeDtypeStruct((n, d), BF16)]
        out_specs = [out_row, out_row]
    else:
        out_shape = [jax.ShapeDtypeStruct((n, d), F32)]
        out_specs = [out_row]
    dr = d // grid[0]
    for w in cast_along:
        assert w.shape in ((d, f), (f, d)) and _can_cast_along(n, d, f, bm, tf)
        spec = (pl.BlockSpec((dr, tf), lambda i, j: (i, fblk(j))) if w.shape == (d, f)
                else pl.BlockSpec((tf, dr), lambda i, j: (fblk(j), i)))
        in_specs.append(spec)
        out_specs.append(spec)
        out_shape.append(jax.ShapeDtypeStruct(w.shape, BF16))
    outs = pl.pallas_call(
        functools.partial(_ffn_kernel, emit_hidden=emit_hidden, n_cast=len(cast_along), n_io=n_io, n_f=n_f),
        grid=grid, in_specs=in_specs, out_specs=out_specs, out_shape=out_shape,
        scratch_shapes=[pltpu.VMEM((bm, d), BF16), pltpu.VMEM((bm, d), F32)],
        compiler_params=_cparams(("parallel", "arbitrary"), 60),
        name="ffn_hidden" if emit_hidden else "ffn_final",
    )(x, g_in.reshape(1, d), wg, wu, wd, g_out.reshape(1, d), *cast_along)
    return outs if len(outs) > 1 else outs[0]


def _rope(y, cos, sin_signed, first_half):
    partner = jnp.where(first_half, pltpu.roll(y, HEAD_DIM - 32, axis=1), pltpu.roll(y, 32, axis=1))
    return y * cos + partner * sin_signed


def _head_pairs(u_ref, w_ref, width):
    u = u_ref[...]
    for c in range(0, width, 2 * HEAD_DIM):
        yield c // HEAD_DIM, _dot(u, w_ref[:, c:c + 2 * HEAD_DIM])


def _proj_q_kernel(u_ref, w_ref, cos_ref, sin_ref, gain_ref, q_ref):
    cos, sin = cos_ref[...], sin_ref[...]
    first_half = (lax.broadcasted_iota(jnp.int32, cos.shape, 1) % 64) < 32
    scale = HEAD_DIM ** -0.5
    for h0, acc in _head_pairs(u_ref, w_ref, q_ref.shape[1]):
        for h in range(2):
            y = _rope(_rms(acc[:, h * HEAD_DIM:(h + 1) * HEAD_DIM], gain_ref[...]), cos, sin, first_half)
            q_ref[:, (h0 + h) * HEAD_DIM:(h0 + h + 1) * HEAD_DIM] = (y * scale).astype(BF16)


def _proj_kv_kernel(u_ref, w_ref, cos_ref, sin_ref, gain_ref, kt_ref, v_ref):
    cos, sin = cos_ref[...], sin_ref[...]
    first_half = (lax.broadcasted_iota(jnp.int32, cos.shape, 1) % 64) < 32
    for h0, acc in _head_pairs(u_ref, w_ref, 2 * A_KV_W):
        for h in range(2):
            part = acc[:, h * HEAD_DIM:(h + 1) * HEAD_DIM]
            if h0 + h < A_KV_HEADS:
                y = _rope(_rms(part, gain_ref[...]), cos, sin, first_half)
                kt_ref[0, (h0 + h) * HEAD_DIM:(h0 + h + 1) * HEAD_DIM, :] = y.T.astype(BF16)
            else:
                hv = h0 + h - A_KV_HEADS
                v_ref[:, hv * HEAD_DIM:(hv + 1) * HEAD_DIM] = part.astype(BF16)


def _rope_tables(t):
    rows = t // GRID_W
    row_ids = jnp.repeat(jnp.arange(rows), GRID_W).astype(F32)
    col_ids = jnp.tile(jnp.arange(GRID_W), rows).astype(F32)
    inv = ROPE_THETA ** (-jnp.arange(0, ROPE_AXIS_DIM, 2, dtype=F32) / ROPE_AXIS_DIM)
    ang_r = row_ids[:, None] * inv[None, :]
    ang_c = col_ids[:, None] * inv[None, :]
    cos = jnp.concatenate([jnp.cos(ang_r)] * 2 + [jnp.cos(ang_c)] * 2, axis=-1)
    sin = jnp.concatenate([-jnp.sin(ang_r), jnp.sin(ang_r), -jnp.sin(ang_c), jnp.sin(ang_c)], axis=-1)
    return cos, sin


def _proj_a(u, w, q_gain, k_gain, *, bsz, bm):
    n, d = u.shape
    t = n // bsz
    assert t % bm == 0 and A_Q_W % PROJ_BN == 0
    tiles_per_seq = t // bm
    cos, sin = _rope_tables(t)
    u_spec = pl.BlockSpec((bm, d), lambda i, j: (i, 0))
    tab = pl.BlockSpec((bm, HEAD_DIM), lambda i, j: (i % tiles_per_seq, 0))
    vec = pl.BlockSpec((1, HEAD_DIM), lambda i, j: (0, 0))
    qp = pl.pallas_call(
        _proj_q_kernel,
        grid=(n // bm, A_Q_W // PROJ_BN),
        in_specs=[u_spec, pl.BlockSpec((d, PROJ_BN), lambda i, j: (0, j)), tab, tab, vec],
        out_specs=pl.BlockSpec((bm, PROJ_BN), lambda i, j: (i, j)),
        out_shape=jax.ShapeDtypeStruct((n, A_Q_W), BF16),
        compiler_params=_cparams(("parallel", "arbitrary"), 48),
        name="proj_q",
    )(u, w[:, :A_Q_W], cos, sin, q_gain.reshape(1, HEAD_DIM))
    kt, v = pl.pallas_call(
        _proj_kv_kernel,
        grid=(n // bm, 1),
        in_specs=[u_spec, pl.BlockSpec((d, 2 * A_KV_W), lambda i, j: (0, 0)), tab, tab, vec],
        out_specs=(pl.BlockSpec((1, A_KV_W, bm), lambda i, j: (i // tiles_per_seq, 0, i % tiles_per_seq)),
                   pl.BlockSpec((bm, A_KV_W), lambda i, j: (i, 0))),
        out_shape=(jax.ShapeDtypeStruct((bsz, A_KV_W, t), BF16), jax.ShapeDtypeStruct((n, A_KV_W), BF16)),
        compiler_params=_cparams(("parallel", "arbitrary"), 48),
        name="proj_kv",
    )(u, w[:, A_Q_W:], cos, sin, k_gain.reshape(1, HEAD_DIM))
    return qp, kt, v


def _proj_cols_kernel(u_ref, w_ref, o_ref, *, sigmoid):
    acc = _dot(u_ref[...], w_ref[...])
    if sigmoid:
        acc = jax.nn.sigmoid(acc)
    o_ref[...] = acc.astype(o_ref.dtype)


def _proj_cols(u, w, *, bm, sigmoid, name):
    n, d = u.shape
    width = w.shape[1]
    bn = _tile(width, PROJ_BN)
    return pl.pallas_call(
        functools.partial(_proj_cols_kernel, sigmoid=sigmoid),
        grid=(n // bm, width // bn),
        in_specs=[pl.BlockSpec((bm, d), lambda i, j: (i, 0)),
                  pl.BlockSpec((d, bn), lambda i, j: (0, j))],
        out_specs=pl.BlockSpec((bm, bn), lambda i, j: (i, j)),
        out_shape=jax.ShapeDtypeStruct((n, width), BF16),
        compiler_params=_cparams(("parallel", "arbitrary"), 48),
        name=name,
    )(u, w)


def _proj_cls_kernel(u_ref, w_ref, o_ref, acc_ref, *, dilation):
    acc = _dot(u_ref[...], w_ref[...])
    rows = o_ref.shape[1]
    for c in range(acc_ref.shape[0]):
        lanes = slice(c * HEAD_DIM, (c + 1) * HEAD_DIM)
        acc_ref[c] = acc[:, lanes]
        for r in range(dilation):
            o_ref[r, :, lanes] = acc_ref[c, pl.ds(r, rows, stride=dilation), :].astype(o_ref.dtype)


def _proj_cls(u, w, *, bm, dilation, name):
    n, d = u.shape
    width = w.shape[1]
    return pl.pallas_call(
        functools.partial(_proj_cls_kernel, dilation=dilation),
        grid=(n // bm, width // PROJ_BN),
        in_specs=[pl.BlockSpec((bm, d), lambda i, j: (i, 0)),
                  pl.BlockSpec((d, PROJ_BN), lambda i, j: (0, j))],
        out_specs=pl.BlockSpec((dilation, bm // dilation, PROJ_BN), lambda i, j: (0, i, j)),
        out_shape=jax.ShapeDtypeStruct((dilation, n // dilation, width), BF16),
        scratch_shapes=[pltpu.VMEM((PROJ_BN // HEAD_DIM, bm, HEAD_DIM), F32)],
        compiler_params=_cparams(("parallel", "arbitrary"), 56),
        name=name,
    )(u, w)


def _attn_a_kernel(q_ref, kt_ref, v_ref, o_ref):
    kt = kt_ref[0]
    v = v_ref[0]
    for r0 in range(0, q_ref.shape[1], ATTN_A_ROWS):
        rows = slice(r0, r0 + ATTN_A_ROWS)
        for g in range(A_GROUP):
            sl = slice(g * HEAD_DIM, (g + 1) * HEAD_DIM)
            s = _dot(q_ref[0, rows, sl], kt)
            p = jnp.exp(s - jnp.max(s, axis=-1, keepdims=True))
            denom = jnp.sum(p, axis=-1, keepdims=True)
            o = _dot(p.astype(BF16), v) / denom
            o_ref[0, rows, sl] = o.astype(o_ref.dtype)


def _attn_a(qp, kt, v, *, tq):
    b, t, _ = qp.shape
    gw = A_GROUP * HEAD_DIM
    assert tq % ATTN_A_ROWS == 0
    return pl.pallas_call(
        _attn_a_kernel,
        grid=(b, A_KV_HEADS, t // tq),
        in_specs=[pl.BlockSpec((1, tq, gw), lambda bi, kv, i: (bi, i, kv)),
                  pl.BlockSpec((1, HEAD_DIM, t), lambda bi, kv, i: (bi, kv, 0)),
                  pl.BlockSpec((1, t, HEAD_DIM), lambda bi, kv, i: (bi, 0, kv))],
        out_specs=pl.BlockSpec((1, tq, gw), lambda bi, kv, i: (bi, i, kv)),
        out_shape=jax.ShapeDtypeStruct((b, t, A_Q_W), BF16),
        compiler_params=_cparams(("parallel", "parallel", "arbitrary"), 48),
        name="attn_a",
    )(qp, kt, v)


def _attn_b_kernel(slopes_ref, q_ref, k_ref, v_ref, o_ref, lse_ref, *, group, dilation, half, tq, kw,
                   heads, interleave):
    length = q_ref.shape[0]
    head0 = group * B_HEADS_PER_GROUP + pl.program_id(1) * heads
    scale = HEAD_DIM ** -0.5
    row = lax.broadcasted_iota(jnp.int32, (tq, kw), 0)
    col = lax.broadcasted_iota(jnp.int32, (tq, kw), 1)

    def tile(i, lanes, slope):
        m0 = pl.multiple_of(i * tq, tq)
        ks = pl.multiple_of(jnp.clip(m0 - half, 0, length - kw), half)
        q = q_ref[pl.ds(m0, tq), lanes]
        k = k_ref[pl.ds(ks, kw), lanes]
        v = v_ref[pl.ds(ks, kw), lanes]
        s = lax.dot_general(q, k, (((1,), (1,)), ((), ())), preferred_element_type=F32) * scale
        dist = jnp.abs(col - row + (ks - m0))
        bias = -slope * (dist * dilation).astype(F32)
        s = jnp.where(dist <= half, s + bias, NEG_INF)
        m = jnp.max(s, axis=-1, keepdims=True)
        p = jnp.exp(s - m)
        denom = jnp.sum(p, axis=-1, keepdims=True)
        o = _dot(p.astype(BF16), v) / denom
        o_ref[pl.ds(m0, tq), lanes] = o.astype(o_ref.dtype)
        lse_ref[pl.ds(m0, tq), lanes] = jnp.broadcast_to(m + jnp.log(denom), (tq, HEAD_DIM))

    for hh in range(heads):
        lanes = slice(hh * HEAD_DIM, (hh + 1) * HEAD_DIM)
        slope = slopes_ref[head0 + hh]

        def body(it, carry, lanes=lanes, slope=slope):
            for u in range(interleave):
                tile(it * interleave + u, lanes, slope)
            return carry

        lax.fori_loop(0, length // (tq * interleave), body, 0)


def _attn_b(qkv, slopes, group, *, bsz):
    window, dilation = B_PATTERNS[group]
    length = qkv.shape[-2] // (1 if dilation == 1 else bsz)
    half = (window // 2) // dilation
    tq = min(128, length)
    kw = min(length, tq + 2 * half)
    heads = 1 if dilation == 1 else B_HEADS_PER_GROUP
    steps = B_HEADS_PER_GROUP // heads
    width = heads * HEAD_DIM

    if dilation == 1:
        in_spec = lambda part: pl.BlockSpec((None, length, width), lambda bi, h, r, s: (bi, 0, part * steps + h))
        out_spec = pl.BlockSpec((None, length, width), lambda bi, h, r, s: (bi, 0, h))
        out_dims = (bsz, length, B_OUT_W)
    else:
        in_spec = lambda part: pl.BlockSpec((None, length, width), lambda bi, h, r, s: (r, bi, part * steps + h))
        out_spec = pl.BlockSpec((None, length, width), lambda bi, h, r, s: (r, bi, h))
        out_dims = (dilation, bsz * length, B_OUT_W)
    n_tiles = length // tq
    return pl.pallas_call(
        functools.partial(_attn_b_kernel, group=group, dilation=dilation, half=half, tq=tq, kw=kw,
                          heads=heads, interleave=min(4, n_tiles)),
        grid_spec=pltpu.PrefetchScalarGridSpec(
            num_scalar_prefetch=1,
            grid=(bsz, steps, dilation),
            in_specs=[in_spec(0), in_spec(1), in_spec(2)],
            out_specs=(out_spec, out_spec)),
        out_shape=(jax.ShapeDtypeStruct(out_dims, BF16), jax.ShapeDtypeStruct(out_dims, F32)),
        compiler_params=_cparams(("parallel", "parallel", "parallel"), 40),
        name=f"attn_b{group}",
    )(slopes, qkv, qkv, qkv)


def _branch_kernel(ya_ref, o0_ref, l0_ref, o1_ref, l1_ref, o2_ref, l2_ref, wa_ref, wb_ref,
                   ga_ref, gb_ref, out_ref, yb_ref, on1_ref, ln1_ref, on2_ref, ln2_ref):
    @pl.when(pl.program_id(1) == 0)
    def _():
        for o_ref, l_ref, on_ref, ln_ref in ((o1_ref, l1_ref, on1_ref, ln1_ref),
                                             (o2_ref, l2_ref, on2_ref, ln2_ref)):
            dilation, rows = o_ref.shape[0], o_ref.shape[1]
            for c in range(on_ref.shape[0]):
                lanes = slice(c * HEAD_DIM, (c + 1) * HEAD_DIM)
                for r in range(dilation):
                    on_ref[c, pl.ds(r, rows, stride=dilation), :] = o_ref[r, :, lanes].astype(F32)
                    ln_ref[c, pl.ds(r, rows, stride=dilation), :] = l_ref[r, :, lanes]
        for c in range(on1_ref.shape[0]):
            lanes = slice(c * HEAD_DIM, (c + 1) * HEAD_DIM)
            l0, l1, l2 = l0_ref[:, lanes], ln1_ref[c], ln2_ref[c]
            m = jnp.maximum(jnp.maximum(l0, l1), l2)
            e0, e1, e2 = jnp.exp(l0 - m), jnp.exp(l1 - m), jnp.exp(l2 - m)
            tot = e0 + e1 + e2
            yb = (e0 / tot) * o0_ref[:, lanes].astype(F32) + (e1 / tot) * on1_ref[c] + (e2 / tot) * on2_ref[c]
            yb_ref[:, lanes] = yb.astype(BF16)

    a = _dot(ya_ref[...], wa_ref[...])
    bb = _dot(yb_ref[...], wb_ref[...])
    merged = ga_ref[...].astype(F32) * a + gb_ref[...].astype(F32) * bb
    out_ref[...] = merged.astype(out_ref.dtype)


def _branch(ya, o_l, wa, wb, gates, *, bm, bn):
    n, d = ya.shape[0], wa.shape[1]
    row = lambda w: pl.BlockSpec((bm, w), lambda i, j: (i, 0))

    def cls(arr):
        dil = arr.shape[0]
        return pl.BlockSpec((dil, bm // dil, B_OUT_W), lambda i, j: (0, i, 0))

    (o0, l0), (o1, l1), (o2, l2) = o_l
    return pl.pallas_call(
        _branch_kernel,
        grid=(n // bm, d // bn),
        in_specs=[row(A_Q_W), row(B_OUT_W), row(B_OUT_W), cls(o1), cls(l1), cls(o2), cls(l2),
                  pl.BlockSpec((A_Q_W, bn), lambda i, j: (0, j)),
                  pl.BlockSpec((B_OUT_W, bn), lambda i, j: (0, j)),
                  pl.BlockSpec((bm, bn), lambda i, j: (i, j)),
                  pl.BlockSpec((bm, bn), lambda i, j: (i, d // bn + j))],
        out_specs=pl.BlockSpec((bm, bn), lambda i, j: (i, j)),
        out_shape=jax.ShapeDtypeStruct((n, d), BF16),
        scratch_shapes=[pltpu.VMEM((bm, B_OUT_W), BF16)]
                       + [pltpu.VMEM((B_HEADS_PER_GROUP, bm, HEAD_DIM), F32)] * 4,
        compiler_params=_cparams(("parallel", "arbitrary"), 56),
        name="branch_merge",
    )(ya, o0, l0, o1, l1, o2, l2, wa, wb, gates, gates)


def _out_kernel(h_ref, a_ref, w_ref, o_ref):
    o_ref[...] = h_ref[...] + _dot(a_ref[...], w_ref[...])


def _out_proj(h, a, w, *, bm, bn):
    n, d = h.shape
    k = a.shape[1]
    return pl.pallas_call(
        _out_kernel,
        grid=(n // bm, d // bn),
        in_specs=[pl.BlockSpec((bm, bn), lambda i, j: (i, j)),
                  pl.BlockSpec((bm, k), lambda i, j: (i, 0)),
                  pl.BlockSpec((k, bn), lambda i, j: (0, j))],
        out_specs=pl.BlockSpec((bm, bn), lambda i, j: (i, j)),
        out_shape=jax.ShapeDtypeStruct((n, d), F32),
        compiler_params=_cparams(("parallel", "arbitrary"), 48),
        name="out_proj",
    )(h, a, w)


def _layer(h, g_ffn1, w1_gate, w1_up, w1_down, g_mix, w_in, q_norm_a, k_norm_a,
           w_branch_a, w_branch_b, w_out, g_ffn2, w2_gate, w2_up, w2_down, g_next, *, bsz, last):
    n, d = h.shape
    t = n // bsz
    c = lambda w: w.astype(BF16)
    bm = _tile(n, 512)
    ffn = functools.partial(_ffn, bm=_tile(n, 1024), tf=_tile(w1_gate.shape[1], 256), io_rows=_tile(n, 128))

    ffn2_w = (w2_gate, w2_up, w2_down)
    if w2_gate.shape == w1_gate.shape and _can_cast_along(n, d, w1_gate.shape[1], ffn.keywords["bm"],
                                                          ffn.keywords["tf"]):
        h, u, *ffn2_w = ffn(h, g_ffn1, c(w1_gate), c(w1_up), c(w1_down), g_mix, emit_hidden=True,
                            cast_along=ffn2_w)
    else:
        h, u = ffn(h, g_ffn1, c(w1_gate), c(w1_up), c(w1_down), g_mix, emit_hidden=True)
        ffn2_w = [c(w) for w in ffn2_w]

    pm = _tile(t, 1024)
    a_w = A_Q_W + 2 * A_KV_W
    b_cols = lambda g: c(w_in[:, a_w + g * B_GROUP_W:a_w + (g + 1) * B_GROUP_W])
    qp, kt, va = _proj_a(u, c(w_in[:, :a_w]), q_norm_a, k_norm_a, bsz=bsz, bm=pm)
    qkv_b = [_proj_cols(u, b_cols(0), bm=pm, sigmoid=False, name="proj_b0").reshape(bsz, t, B_GROUP_W)]
    for g in range(1, B_N_GROUPS):
        qkv_b.append(_proj_cls(u, b_cols(g), bm=pm, dilation=B_PATTERNS[g][1], name=f"proj_b{g}"))
    gates = _proj_cols(u, c(w_in[:, a_w + B_QKV_W:]), bm=pm, sigmoid=True, name="proj_gates")

    ya = _attn_a(qp.reshape(bsz, t, A_Q_W), kt, va.reshape(bsz, t, A_KV_W), tq=_tile(t, 1024)).reshape(n, A_Q_W)

    slopes = jnp.exp2(-8.0 * jnp.arange(1, B_HEADS + 1, dtype=F32) / B_HEADS)
    o_l = [_attn_b(qkv_b[g], slopes, g, bsz=bsz) for g in range(B_N_GROUPS)]
    o_l[0] = tuple(a.reshape(n, B_OUT_W) for a in o_l[0])

    merged = _branch(ya, o_l, c(w_branch_a), c(w_branch_b), gates, bm=bm, bn=_tile(d, 1024))
    h = _out_proj(h, merged, c(w_out), bm=_tile(n, 1024), bn=_tile(d, 512))

    if last:
        return ffn(h, g_ffn2, *ffn2_w, g_next, emit_hidden=False)
    return ffn(h, g_ffn2, *ffn2_w, g_next, emit_hidden=True)[0]


def kernel(x, g_ffn1, w1_gate, w1_up, w1_down, g_mix, w_in, q_norm_a, k_norm_a, w_branch_a, w_branch_b,
           w_out, g_ffn2, w2_gate, w2_up, w2_down, g_final):
    bsz, t, d = x.shape
    depth = g_ffn1.shape[0]
    h = x.reshape(bsz * t, d)
    for l in range(depth):
        last = l == depth - 1
        h = _layer(h, g_ffn1[l], w1_gate[l], w1_up[l], w1_down[l], g_mix[l], w_in[l], q_norm_a[l],
                   k_norm_a[l], w_branch_a[l], w_branch_b[l], w_out[l], g_ffn2[l], w2_gate[l], w2_up[l],
                   w2_down[l], g_final if last else g_ffn1[l + 1], bsz=bsz, last=last)
    return h.reshape(bsz, t, d)
```

```python
import functools

import jax
import jax.numpy as jnp
from jax import lax
from jax.experimental import pallas as pl
from jax.experimental.pallas import tpu as pltpu

HEAD_DIM = 128
A_HEADS = 16
A_KV_HEADS = 4
A_GROUP = A_HEADS // A_KV_HEADS
B_PATTERNS = ((128, 1), (512, 4), (2048, 16))
B_HEADS_PER_GROUP = 8
B_N_GROUPS = len(B_PATTERNS)
B_HEADS = B_N_GROUPS * B_HEADS_PER_GROUP
A_Q_W = A_HEADS * HEAD_DIM
A_KV_W = A_KV_HEADS * HEAD_DIM
B_OUT_W = B_HEADS_PER_GROUP * HEAD_DIM
B_GROUP_W = 3 * B_OUT_W
B_QKV_W = B_N_GROUPS * B_GROUP_W
GRID_W = 64
ROPE_THETA = 10000.0
ROPE_AXIS_DIM = HEAD_DIM // 2
RMS_EPS = 1e-6
NEG_INF = -1e30

DOWN_CHUNK = 512
FFN_MM_ROWS = 512
ATTN_A_ROWS = 256
PROJ_BN = 2 * A_KV_W
MIB = 1024 * 1024
BF16 = jnp.bfloat16
F32 = jnp.float32


def _cparams(sem, vmem_mib):
    return pltpu.CompilerParams(dimension_semantics=sem, vmem_limit_bytes=vmem_mib * MIB)


def _tile(n, pref):
    t = min(n, pref)
    while n % t:
        t //= 2
    return t


def _rms(x, gain):
    y = x * lax.rsqrt(jnp.mean(x * x, axis=-1, keepdims=True) + RMS_EPS)
    return y * gain


def _dot(a, b):
    return jnp.dot(a, b, preferred_element_type=F32)


def _ffn_kernel(x_ref, gin_ref, wg_ref, wu_ref, wd_ref, gout_ref, *rest, emit_hidden, n_cast, n_io, n_f):
    cast_in, rest = rest[:n_cast], rest[n_cast:]
    outs, rest = rest[:2 if emit_hidden else 1], rest[2 if emit_hidden else 1:]
    cast_out, (xn_ref, acc_ref) = rest[:n_cast], rest[n_cast:]
    y_ref = outs[-1]
    io_rows = x_ref.shape[0]
    j = pl.program_id(1)
    first_mm, last_mm = n_io - 1, n_io + n_f - 2

    for src_ref, dst_ref in zip(cast_in, cast_out):
        dst_ref[...] = src_ref[...].astype(dst_ref.dtype)

    @pl.when(j <= first_mm)
    def _():
        rows = pl.ds(pl.multiple_of(j * io_rows, io_rows), io_rows)
        x = x_ref[...]
        xn_ref[rows, :] = _rms(x, gin_ref[...]).astype(BF16)
        acc_ref[rows, :] = x

    @pl.when((j >= first_mm) & (j <= last_mm))
    def _():
        for r0 in range(0, acc_ref.shape[0], FFN_MM_ROWS):
            rows = slice(r0, r0 + FFN_MM_ROWS)
            xn = xn_ref[rows, :]
            gate = _dot(xn, wg_ref[...])
            up = _dot(xn, wu_ref[...])
            act = (jax.nn.silu(gate) * up * 0.5).astype(BF16)
            for c in range(0, acc_ref.shape[1], DOWN_CHUNK):
                acc_ref[rows, c:c + DOWN_CHUNK] += _dot(act, wd_ref[:, c:c + DOWN_CHUNK])

    @pl.when(j >= last_mm)
    def _():
        rows = pl.ds(pl.multiple_of((j - last_mm) * io_rows, io_rows), io_rows)
        h = acc_ref[rows, :]
        if emit_hidden:
            outs[0][...] = h
        y_ref[...] = _rms(h, gout_ref[...]).astype(y_ref.dtype)


def _can_cast_along(n, d, f, bm, tf):
    row_tiles = n // bm
    return d % row_tiles == 0 and (d // row_tiles) % HEAD_DIM == 0 and f % tf == 0


def _ffn(x, g_in, wg, wu, wd, g_out, *, emit_hidden, bm, tf, io_rows, cast_along=()):
    n, d = x.shape
    f = wg.shape[1]
    n_io, n_f = bm // io_rows, f // tf
    assert bm % FFN_MM_ROWS == 0 and bm % io_rows == 0
    grid = (n // bm, n_f + 2 * (n_io - 1))
    fblk = lambda j: jnp.clip(j - (n_io - 1), 0, n_f - 1)
    vec = pl.BlockSpec((1, d), lambda i, j: (0, 0))
    in_specs = [pl.BlockSpec((io_rows, d), lambda i, j: (i * n_io + jnp.minimum(j, n_io - 1), 0)), vec,
                pl.BlockSpec((d, tf), lambda i, j: (0, fblk(j))),
                pl.BlockSpec((d, tf), lambda i, j: (0, fblk(j))),
                pl.BlockSpec((tf, d), lambda i, j: (fblk(j), 0)),
                vec]
    out_row = pl.BlockSpec((io_rows, d), lambda i, j: (i * n_io + jnp.clip(j - (n_io + n_f - 2), 0, n_io - 1), 0))
    if emit_hidden:
        out_shape = [jax.ShapeDtypeStruct((n, d), F32), jax.ShapeDtypeStruct((n, d), BF16)]
        out_specs = [out_row, out_row]
    else:
        out_shape = [jax.ShapeDtypeStruct((n, d), F32)]
        out_specs = [out_row]
    dr = d // grid[0]
    for w in cast_along:
        assert w.shape in ((d, f), (f, d)) and _can_cast_along(n, d, f, bm, tf)
        spec = (pl.BlockSpec((dr, tf), lambda i, j: (i, fblk(j))) if w.shape == (d, f)
                else pl.BlockSpec((tf, dr), lambda i, j: (fblk(j), i)))
        in_specs.append(spec)
        out_specs.append(spec)
        out_shape.append(jax.ShapeDtypeStruct(w.shape, BF16))
    outs = pl.pallas_call(
        functools.partial(_ffn_kernel, emit_hidden=emit_hidden, n_cast=len(cast_along), n_io=n_io, n_f=n_f),
        grid=grid, in_specs=in_specs, out_specs=out_specs, out_shape=out_shape,
        scratch_shapes=[pltpu.VMEM((bm, d), BF16), pltpu.VMEM((bm, d), F32)],
        compiler_params=_cparams(("parallel", "arbitrary"), 60),
        name="ffn_hidden" if emit_hidden else "ffn_final",
    )(x, g_in.reshape(1, d), wg, wu, wd, g_out.reshape(1, d), *cast_along)
    return outs if len(outs) > 1 else outs[0]


def _rope(y, cos, sin_signed, first_half):
    partner = jnp.where(first_half, pltpu.roll(y, HEAD_DIM - 32, axis=1), pltpu.roll(y, 32, axis=1))
    return y * cos + partner * sin_signed


def _head_pairs(u_ref, w_ref, width):
    u = u_ref[...]
    for c in range(0, width, 2 * HEAD_DIM):
        yield c // HEAD_DIM, _dot(u, w_ref[:, c:c + 2 * HEAD_DIM])


def _proj_q_kernel(u_ref, w_ref, cos_ref, sin_ref, gain_ref, q_ref):
    cos, sin = cos_ref[...], sin_ref[...]
    first_half = (lax.broadcasted_iota(jnp.int32, cos.shape, 1) % 64) < 32
    scale = HEAD_DIM ** -0.5
    for h0, acc in _head_pairs(u_ref, w_ref, q_ref.shape[1]):
        for h in range(2):
            y = _rope(_rms(acc[:, h * HEAD_DIM:(h + 1) * HEAD_DIM], gain_ref[...]), cos, sin, first_half)
            q_ref[:, (h0 + h) * HEAD_DIM:(h0 + h + 1) * HEAD_DIM] = (y * scale).astype(BF16)


def _proj_kv_kernel(u_ref, w_ref, cos_ref, sin_ref, gain_ref, kt_ref, v_ref):
    cos, sin = cos_ref[...], sin_ref[...]
    first_half = (lax.broadcasted_iota(jnp.int32, cos.shape, 1) % 64) < 32
    for h0, acc in _head_pairs(u_ref, w_ref, 2 * A_KV_W):
        for h in range(2):
            part = acc[:, h * HEAD_DIM:(h + 1) * HEAD_DIM]
            if h0 + h < A_KV_HEADS:
                y = _rope(_rms(part, gain_ref[...]), cos, sin, first_half)
                kt_ref[0, (h0 + h) * HEAD_DIM:(h0 + h + 1) * HEAD_DIM, :] = y.T.astype(BF16)
            else:
                hv = h0 + h - A_KV_HEADS
                v_ref[:, hv * HEAD_DIM:(hv + 1) * HEAD_DIM] = part.astype(BF16)


def _rope_tables(t):
    rows = t // GRID_W
    row_ids = jnp.repeat(jnp.arange(rows), GRID_W).astype(F32)
    col_ids = jnp.tile(jnp.arange(GRID_W), rows).astype(F32)
    inv = ROPE_THETA ** (-jnp.arange(0, ROPE_AXIS_DIM, 2, dtype=F32) / ROPE_AXIS_DIM)
    ang_r = row_ids[:, None] * inv[None, :]
    ang_c = col_ids[:, None] * inv[None, :]
    cos = jnp.concatenate([jnp.cos(ang_r)] * 2 + [jnp.cos(ang_c)] * 2, axis=-1)
    sin = jnp.concatenate([-jnp.sin(ang_r), jnp.sin(ang_r), -jnp.sin(ang_c), jnp.sin(ang_c)], axis=-1)
    return cos, sin


def _proj_a(u, w, q_gain, k_gain, *, bsz, bm):
    n, d = u.shape
    t = n // bsz
    assert t % bm == 0 and A_Q_W % PROJ_BN == 0
    tiles_per_seq = t // bm
    cos, sin = _rope_tables(t)
    u_spec = pl.BlockSpec((bm, d), lambda i, j: (i, 0))
    tab = pl.BlockSpec((bm, HEAD_DIM), lambda i, j: (i % tiles_per_seq, 0))
    vec = pl.BlockSpec((1, HEAD_DIM), lambda i, j: (0, 0))
    qp = pl.pallas_call(
        _proj_q_kernel,
        grid=(n // bm, A_Q_W // PROJ_BN),
        in_specs=[u_spec, pl.BlockSpec((d, PROJ_BN), lambda i, j: (0, j)), tab, tab, vec],
        out_specs=pl.BlockSpec((bm, PROJ_BN), lambda i, j: (i, j)),
        out_shape=jax.ShapeDtypeStruct((n, A_Q_W), BF16),
        compiler_params=_cparams(("parallel", "arbitrary"), 48),
        name="proj_q",
    )(u, w, cos, sin, q_gain.reshape(1, HEAD_DIM))
    kt, v = pl.pallas_call(
        _proj_kv_kernel,
        grid=(n // bm, 1),
        in_specs=[u_spec, pl.BlockSpec((d, 2 * A_KV_W), lambda i, j: (0, A_Q_W // (2 * A_KV_W))), tab, tab, vec],
        out_specs=(pl.BlockSpec((1, A_KV_W, bm), lambda i, j: (i // tiles_per_seq, 0, i % tiles_per_seq)),
                   pl.BlockSpec((bm, A_KV_W), lambda i, j: (i, 0))),
        out_shape=(jax.ShapeDtypeStruct((bsz, A_KV_W, t), BF16), jax.ShapeDtypeStruct((n, A_KV_W), BF16)),
        compiler_params=_cparams(("parallel", "arbitrary"), 48),
        name="proj_kv",
    )(u, w, cos, sin, k_gain.reshape(1, HEAD_DIM))
    return qp, kt, v


def _proj_cols_kernel(u_ref, w_ref, o_ref, *, sigmoid):
    acc = _dot(u_ref[...], w_ref[...])
    if sigmoid:
        acc = jax.nn.sigmoid(acc)
    o_ref[...] = acc.astype(o_ref.dtype)


def _proj_cols(u, w, *, col0, width, bm, sigmoid, name):
    n, d = u.shape
    bn = _tile(width, PROJ_BN)
    assert col0 % bn == 0
    return pl.pallas_call(
        functools.partial(_proj_cols_kernel, sigmoid=sigmoid),
        grid=(n // bm, width // bn),
        in_specs=[pl.BlockSpec((bm, d), lambda i, j: (i, 0)),
                  pl.BlockSpec((d, bn), lambda i, j: (0, col0 // bn + j))],
        out_specs=pl.BlockSpec((bm, bn), lambda i, j: (i, j)),
        out_shape=jax.ShapeDtypeStruct((n, width), BF16),
        compiler_params=_cparams(("parallel", "arbitrary"), 48),
        name=name,
    )(u, w)


def _proj_cls_kernel(u_ref, w_ref, o_ref, acc_ref, *, dilation):
    acc = _dot(u_ref[...], w_ref[...])
    rows = o_ref.shape[1]
    for c in range(acc_ref.shape[0]):
        lanes = slice(c * HEAD_DIM, (c + 1) * HEAD_DIM)
        acc_ref[c] = acc[:, lanes]
        for r in range(dilation):
            o_ref[r, :, lanes] = acc_ref[c, pl.ds(r, rows, stride=dilation), :].astype(o_ref.dtype)


def _proj_cls(u, w, *, col0, width, bm, dilation, name):
    n, d = u.shape
    assert col0 % PROJ_BN == 0
    return pl.pallas_call(
        functools.partial(_proj_cls_kernel, dilation=dilation),
        grid=(n // bm, width // PROJ_BN),
        in_specs=[pl.BlockSpec((bm, d), lambda i, j: (i, 0)),
                  pl.BlockSpec((d, PROJ_BN), lambda i, j: (0, col0 // PROJ_BN + j))],
        out_specs=pl.BlockSpec((dilation, bm // dilation, PROJ_BN), lambda i, j: (0, i, j)),
        out_shape=jax.ShapeDtypeStruct((dilation, n // dilation, width), BF16),
        scratch_shapes=[pltpu.VMEM((PROJ_BN // HEAD_DIM, bm, HEAD_DIM), F32)],
        compiler_params=_cparams(("parallel", "arbitrary"), 56),
        name=name,
    )(u, w)


def _attn_a_kernel(q_ref, kt_ref, v_ref, o_ref):
    kt = kt_ref[0]
    v = v_ref[0]
    for r0 in range(0, q_ref.shape[1], ATTN_A_ROWS):
        rows = slice(r0, r0 + ATTN_A_ROWS)
        for g in range(A_GROUP):
            sl = slice(g * HEAD_DIM, (g + 1) * HEAD_DIM)
            s = _dot(q_ref[0, rows, sl], kt)
            p = jnp.exp(s - jnp.max(s, axis=-1, keepdims=True))
            denom = jnp.sum(p, axis=-1, keepdims=True)
            o = _dot(p.astype(BF16), v) / denom
            o_ref[0, rows, sl] = o.astype(o_ref.dtype)


def _attn_a(qp, kt, v, *, tq):
    b, t, _ = qp.shape
    gw = A_GROUP * HEAD_DIM
    assert tq % ATTN_A_ROWS == 0
    return pl.pallas_call(
        _attn_a_kernel,
        grid=(b, A_KV_HEADS, t // tq),
        in_specs=[pl.BlockSpec((1, tq, gw), lambda bi, kv, i: (bi, i, kv)),
                  pl.BlockSpec((1, HEAD_DIM, t), lambda bi, kv, i: (bi, kv, 0)),
                  pl.BlockSpec((1, t, HEAD_DIM), lambda bi, kv, i: (bi, 0, kv))],
        out_specs=pl.BlockSpec((1, tq, gw), lambda bi, kv, i: (bi, i, kv)),
        out_shape=jax.ShapeDtypeStruct((b, t, A_Q_W), BF16),
        compiler_params=_cparams(("parallel", "parallel", "arbitrary"), 48),
        name="attn_a",
    )(qp, kt, v)


def _attn_b_kernel(slopes_ref, q_ref, k_ref, v_ref, o_ref, lse_ref, *, group, dilation, half, tq, kw,
                   heads, interleave):
    length = q_ref.shape[0]
    head0 = group * B_HEADS_PER_GROUP + pl.program_id(1) * heads
    scale = HEAD_DIM ** -0.5
    row = lax.broadcasted_iota(jnp.int32, (tq, kw), 0)
    col = lax.broadcasted_iota(jnp.int32, (tq, kw), 1)

    def tile(i, lanes, slope):
        m0 = pl.multiple_of(i * tq, tq)
        ks = pl.multiple_of(jnp.clip(m0 - half, 0, length - kw), half)
        q = q_ref[pl.ds(m0, tq), lanes]
        k = k_ref[pl.ds(ks, kw), lanes]
        v = v_ref[pl.ds(ks, kw), lanes]
        s = lax.dot_general(q, k, (((1,), (1,)), ((), ())), preferred_element_type=F32) * scale
        dist = jnp.abs(col - row + (ks - m0))
        bias = -slope * (dist * dilation).astype(F32)
        s = jnp.where(dist <= half, s + bias, NEG_INF)
        m = jnp.max(s, axis=-1, keepdims=True)
        p = jnp.exp(s - m)
        denom = jnp.sum(p, axis=-1, keepdims=True)
        o = _dot(p.astype(BF16), v) / denom
        o_ref[pl.ds(m0, tq), lanes] = o.astype(o_ref.dtype)
        lse_ref[pl.ds(m0, tq), lanes] = jnp.broadcast_to(m + jnp.log(denom), (tq, HEAD_DIM))

    for hh in range(heads):
        lanes = slice(hh * HEAD_DIM, (hh + 1) * HEAD_DIM)
        slope = slopes_ref[head0 + hh]

        def body(it, carry, lanes=lanes, slope=slope):
            for u in range(interleave):
                tile(it * interleave + u, lanes, slope)
            return carry

        lax.fori_loop(0, length // (tq * interleave), body, 0)


def _attn_b(qkv, slopes, group, *, bsz):
    window, dilation = B_PATTERNS[group]
    length = qkv.shape[-2] // (1 if dilation == 1 else bsz)
    half = (window // 2) // dilation
    tq = min(128, length)
    kw = min(length, tq + 2 * half)
    heads = 1 if dilation == 1 else B_HEADS_PER_GROUP
    steps = B_HEADS_PER_GROUP // heads
    width = heads * HEAD_DIM

    if dilation == 1:
        in_spec = lambda part: pl.BlockSpec((None, length, width), lambda bi, h, r, s: (bi, 0, part * steps + h))
        out_spec = pl.BlockSpec((None, length, width), lambda bi, h, r, s: (bi, 0, h))
        out_dims = (bsz, length, B_OUT_W)
    else:
        in_spec = lambda part: pl.BlockSpec((None, length, width), lambda bi, h, r, s: (r, bi, part * steps + h))
        out_spec = pl.BlockSpec((None, length, width), lambda bi, h, r, s: (r, bi, h))
        out_dims = (dilation, bsz * length, B_OUT_W)
    n_tiles = length // tq
    return pl.pallas_call(
        functools.partial(_attn_b_kernel, group=group, dilation=dilation, half=half, tq=tq, kw=kw,
                          heads=heads, interleave=min(4, n_tiles)),
        grid_spec=pltpu.PrefetchScalarGridSpec(
            num_scalar_prefetch=1,
            grid=(bsz, steps, dilation),
            in_specs=[in_spec(0), in_spec(1), in_spec(2)],
            out_specs=(out_spec, out_spec)),
        out_shape=(jax.ShapeDtypeStruct(out_dims, BF16), jax.ShapeDtypeStruct(out_dims, F32)),
        compiler_params=_cparams(("parallel", "parallel", "parallel"), 40),
        name=f"attn_b{group}",
    )(slopes, qkv, qkv, qkv)


def _merge_b_kernel(o0_ref, l0_ref, o1_ref, l1_ref, o2_ref, l2_ref, yb_ref, on1_ref, ln1_ref, on2_ref, ln2_ref):
    for o_ref, l_ref, on_ref, ln_ref in ((o1_ref, l1_ref, on1_ref, ln1_ref),
                                         (o2_ref, l2_ref, on2_ref, ln2_ref)):
        dilation, rows = o_ref.shape[0], o_ref.shape[1]
        for c in range(on_ref.shape[0]):
            lanes = slice(c * HEAD_DIM, (c + 1) * HEAD_DIM)
            for r in range(dilation):
                on_ref[c, pl.ds(r, rows, stride=dilation), :] = o_ref[r, :, lanes].astype(F32)
                ln_ref[c, pl.ds(r, rows, stride=dilation), :] = l_ref[r, :, lanes]
    for c in range(on1_ref.shape[0]):
        lanes = slice(c * HEAD_DIM, (c + 1) * HEAD_DIM)
        l0, l1, l2 = l0_ref[:, lanes], ln1_ref[c], ln2_ref[c]
        m = jnp.maximum(jnp.maximum(l0, l1), l2)
        e0, e1, e2 = jnp.exp(l0 - m), jnp.exp(l1 - m), jnp.exp(l2 - m)
        tot = e0 + e1 + e2
        yb = (e0 / tot) * o0_ref[:, lanes].astype(F32) + (e1 / tot) * on1_ref[c] + (e2 / tot) * on2_ref[c]
        yb_ref[:, lanes] = yb.astype(BF16)


def _merge_b(o_l, *, bm):
    (o0, l0), (o1, l1), (o2, l2) = o_l
    n = o0.shape[0]
    row = pl.BlockSpec((bm, B_OUT_W), lambda i: (i, 0))

    def cls(arr):
        dil = arr.shape[0]
        return pl.BlockSpec((dil, bm // dil, B_OUT_W), lambda i: (0, i, 0))

    return pl.pallas_call(
        _merge_b_kernel,
        grid=(n // bm,),
        in_specs=[row, row, cls(o1), cls(l1), cls(o2), cls(l2)],
        out_specs=row,
        out_shape=jax.ShapeDtypeStruct((n, B_OUT_W), BF16),
        scratch_shapes=[pltpu.VMEM((B_HEADS_PER_GROUP, bm, HEAD_DIM), F32)] * 4,
        compiler_params=_cparams(("parallel",), 40),
        name="merge_b",
    )(o0, l0, o1, l1, o2, l2)


def _branch_kernel(ya_ref, yb_ref, wa_ref, wb_ref, ga_ref, gb_ref, out_ref):
    a = _dot(ya_ref[...], wa_ref[...])
    bb = _dot(yb_ref[...], wb_ref[...])
    merged = ga_ref[...].astype(F32) * a + gb_ref[...].astype(F32) * bb
    out_ref[...] = merged.astype(out_ref.dtype)


def _branch(ya, yb, wa, wb, gates, *, bm, bn):
    n, d = ya.shape[0], wa.shape[1]
    row = lambda w: pl.BlockSpec((bm, w), lambda i, j: (i, 0))
    return pl.pallas_call(
        _branch_kernel,
        grid=(n // bm, d // bn),
        in_specs=[row(A_Q_W), row(B_OUT_W),
                  pl.BlockSpec((A_Q_W, bn), lambda i, j: (0, j)),
                  pl.BlockSpec((B_OUT_W, bn), lambda i, j: (0, j)),
                  pl.BlockSpec((bm, bn), lambda i, j: (i, j)),
                  pl.BlockSpec((bm, bn), lambda i, j: (i, d // bn + j))],
        out_specs=pl.BlockSpec((bm, bn), lambda i, j: (i, j)),
        out_shape=jax.ShapeDtypeStruct((n, d), BF16),
        compiler_params=_cparams(("parallel", "arbitrary"), 48),
        name="branch_proj",
    )(ya, yb, wa, wb, gates, gates)


def _out_kernel(h_ref, a_ref, w_ref, o_ref):
    o_ref[...] = h_ref[...] + _dot(a_ref[...], w_ref[...])


def _out_proj(h, a, w, *, bm, bn):
    n, d = h.shape
    k = a.shape[1]
    return pl.pallas_call(
        _out_kernel,
        grid=(n // bm, d // bn),
        in_specs=[pl.BlockSpec((bm, bn), lambda i, j: (i, j)),
                  pl.BlockSpec((bm, k), lambda i, j: (i, 0)),
                  pl.BlockSpec((k, bn), lambda i, j: (0, j))],
        out_specs=pl.BlockSpec((bm, bn), lambda i, j: (i, j)),
        out_shape=jax.ShapeDtypeStruct((n, d), F32),
        compiler_params=_cparams(("parallel", "arbitrary"), 48),
        name="out_proj",
    )(h, a, w)


def _layer(h, g_ffn1, w1_gate, w1_up, w1_down, g_mix, w_in, q_norm_a, k_norm_a,
           w_branch_a, w_branch_b, w_out, g_ffn2, w2_gate, w2_up, w2_down, g_next, *, bsz, last):
    n, d = h.shape
    t = n // bsz
    c = lambda w: w.astype(BF16)
    bm = _tile(n, 512)
    ffn = functools.partial(_ffn, bm=_tile(n, 1024), tf=_tile(w1_gate.shape[1], 256), io_rows=_tile(n, 128))

    ffn2_w = (w2_gate, w2_up, w2_down)
    if w2_gate.shape == w1_gate.shape and _can_cast_along(n, d, w1_gate.shape[1], ffn.keywords["bm"],
                                                          ffn.keywords["tf"]):
        h, u, *ffn2_w = ffn(h, g_ffn1, c(w1_gate), c(w1_up), c(w1_down), g_mix, emit_hidden=True,
                            cast_along=ffn2_w)
    else:
        h, u = ffn(h, g_ffn1, c(w1_gate), c(w1_up), c(w1_down), g_mix, emit_hidden=True)
        ffn2_w = [c(w) for w in ffn2_w]

    pm = _tile(t, 1024)
    a_w = A_Q_W + 2 * A_KV_W
    w_in = c(w_in)
    qp, kt, va = _proj_a(u, w_in, q_norm_a, k_norm_a, bsz=bsz, bm=pm)
    qkv_b = [_proj_cols(u, w_in, col0=a_w, width=B_GROUP_W, bm=pm, sigmoid=False,
                        name="proj_b0").reshape(bsz, t, B_GROUP_W)]
    for g in range(1, B_N_GROUPS):
        qkv_b.append(_proj_cls(u, w_in, col0=a_w + g * B_GROUP_W, width=B_GROUP_W, bm=pm,
                               dilation=B_PATTERNS[g][1], name=f"proj_b{g}"))
    gates = _proj_cols(u, w_in, col0=a_w + B_QKV_W, width=2 * d, bm=pm, sigmoid=True, name="proj_gates")

    ya = _attn_a(qp.reshape(bsz, t, A_Q_W), kt, va.reshape(bsz, t, A_KV_W), tq=_tile(t, 1024)).reshape(n, A_Q_W)

    slopes = jnp.exp2(-8.0 * jnp.arange(1, B_HEADS + 1, dtype=F32) / B_HEADS)
    o_l = [_attn_b(qkv_b[g], slopes, g, bsz=bsz) for g in range(B_N_GROUPS)]
    o_l[0] = tuple(a.reshape(n, B_OUT_W) for a in o_l[0])

    yb = _merge_b(o_l, bm=bm)
    merged = _branch(ya, yb, c(w_branch_a), c(w_branch_b), gates, bm=_tile(n, 1024), bn=_tile(d, 1024))
    h = _out_proj(h, merged, c(w_out), bm=_tile(n, 1024), bn=_tile(d, 512))

    if last:
        return ffn(h, g_ffn2, *ffn2_w, g_next, emit_hidden=False)
    return ffn(h, g_ffn2, *ffn2_w, g_next, emit_hidden=True)[0]


def kernel(x, g_ffn1, w1_gate, w1_up, w1_down, g_mix, w_in, q_norm_a, k_norm_a, w_branch_a, w_branch_b,
           w_out, g_ffn2, w2_gate, w2_up, w2_down, g_final):
    bsz, t, d = x.shape
    depth = g_ffn1.shape[0]
    h = x.reshape(bsz * t, d)
    for l in range(depth):
        last = l == depth - 1
        h = _layer(h, g_ffn1[l], w1_gate[l], w1_up[l], w1_down[l], g_mix[l], w_in[l], q_norm_a[l],
                   k_norm_a[l], w_branch_a[l], w_branch_b[l], w_out[l], g_ffn2[l], w2_gate[l], w2_up[l],
                   w2_down[l], g_final if last else g_ffn1[l + 1], bsz=bsz, last=last)
    return h.reshape(bsz, t, d)
```

```python
import functools

import jax
import jax.numpy as jnp
from jax import lax
from jax.experimental import pallas as pl
from jax.experimental.pallas import tpu as pltpu

HEAD_DIM = 128
A_HEADS = 16
A_KV_HEADS = 4
A_GROUP = A_HEADS // A_KV_HEADS
B_PATTERNS = ((128, 1), (512, 4), (2048, 16))
B_HEADS_PER_GROUP = 8
B_N_GROUPS = len(B_PATTERNS)
B_HEADS = B_N_GROUPS * B_HEADS_PER_GROUP
A_Q_W = A_HEADS * HEAD_DIM
A_KV_W = A_KV_HEADS * HEAD_DIM
B_OUT_W = B_HEADS_PER_GROUP * HEAD_DIM
B_GROUP_W = 3 * B_OUT_W
B_QKV_W = B_N_GROUPS * B_GROUP_W
GRID_W = 64
ROPE_THETA = 10000.0
ROPE_AXIS_DIM = HEAD_DIM // 2
RMS_EPS = 1e-6
NEG_INF = -1e30

DOWN_CHUNK = 512
FFN_MM_ROWS = 512
ATTN_A_ROWS = 256
PROJ_BN = 2 * A_KV_W
MIB = 1024 * 1024
BF16 = jnp.bfloat16
F32 = jnp.float32


def _cparams(sem, vmem_mib):
    return pltpu.CompilerParams(dimension_semantics=sem, vmem_limit_bytes=vmem_mib * MIB)


def _tile(n, pref):
    t = min(n, pref)
    while n % t:
        t //= 2
    return t


def _rms(x, gain):
    y = x * lax.rsqrt(jnp.mean(x * x, axis=-1, keepdims=True) + RMS_EPS)
    return y * gain


def _dot(a, b):
    return jnp.dot(a, b, preferred_element_type=F32)


def _ffn_kernel(x_ref, gin_ref, wg_ref, wu_ref, wd_ref, gout_ref, *rest, emit_hidden, n_cast, n_io, n_f):
    cast_in, rest = rest[:n_cast], rest[n_cast:]
    outs, rest = rest[:2 if emit_hidden else 1], rest[2 if emit_hidden else 1:]
    cast_out, (xn_ref, acc_ref) = rest[:n_cast], rest[n_cast:]
    y_ref = outs[-1]
    io_rows = x_ref.shape[0]
    j = pl.program_id(1)
    first_mm, last_mm = n_io - 1, n_io + n_f - 2

    for src_ref, dst_ref in zip(cast_in, cast_out):
        dst_ref[...] = src_ref[...].astype(dst_ref.dtype)

    @pl.when(j <= first_mm)
    def _():
        rows = pl.ds(pl.multiple_of(j * io_rows, io_rows), io_rows)
        x = x_ref[...]
        xn_ref[rows, :] = _rms(x, gin_ref[...]).astype(BF16)
        acc_ref[rows, :] = x

    @pl.when((j >= first_mm) & (j <= last_mm))
    def _():
        for r0 in range(0, acc_ref.shape[0], FFN_MM_ROWS):
            rows = slice(r0, r0 + FFN_MM_ROWS)
            xn = xn_ref[rows, :]
            gate = _dot(xn, wg_ref[...])
            up = _dot(xn, wu_ref[...])
            act = (jax.nn.silu(gate) * up * 0.5).astype(BF16)
            for c in range(0, acc_ref.shape[1], DOWN_CHUNK):
                acc_ref[rows, c:c + DOWN_CHUNK] += _dot(act, wd_ref[:, c:c + DOWN_CHUNK])

    @pl.when(j >= last_mm)
    def _():
        rows = pl.ds(pl.multiple_of((j - last_mm) * io_rows, io_rows), io_rows)
        h = acc_ref[rows, :]
        if emit_hidden:
            outs[0][...] = h
        y_ref[...] = _rms(h, gout_ref[...]).astype(y_ref.dtype)


def _can_cast_along(n, d, f, bm, tf):
    row_tiles = n // bm
    return d % row_tiles == 0 and (d // row_tiles) % HEAD_DIM == 0 and f % tf == 0


def _ffn(x, g_in, wg, wu, wd, g_out, *, emit_hidden, bm, tf, io_rows, cast_along=()):
    n, d = x.shape
    f = wg.shape[1]
    n_io, n_f = bm // io_rows, f // tf
    assert bm % FFN_MM_ROWS == 0 and bm % io_rows == 0
    grid = (n // bm, n_f + 2 * (n_io - 1))
    fblk = lambda j: jnp.clip(j - (n_io - 1), 0, n_f - 1)
    vec = pl.BlockSpec((1, d), lambda i, j: (0, 0))
    in_specs = [pl.BlockSpec((io_rows, d), lambda i, j: (i * n_io + jnp.minimum(j, n_io - 1), 0)), vec,
                pl.BlockSpec((d, tf), lambda i, j: (0, fblk(j))),
                pl.BlockSpec((d, tf), lambda i, j: (0, fblk(j))),
                pl.BlockSpec((tf, d), lambda i, j: (fblk(j), 0)),
                vec]
    out_row = pl.BlockSpec((io_rows, d), lambda i, j: (i * n_io + jnp.clip(j - (n_io + n_f - 2), 0, n_io - 1), 0))
    if emit_hidden:
        out_shape = [jax.ShapeDtypeStruct((n, d), F32), jax.ShapeDtypeStruct((n, d), BF16)]
        out_specs = [out_row, out_row]
    else:
        out_shape = [jax.ShapeDtypeStruct((n, d), F32)]
        out_specs = [out_row]
    dr = d // grid[0]
    for w in cast_along:
        assert w.shape in ((d, f), (f, d)) and _can_cast_along(n, d, f, bm, tf)
        spec = (pl.BlockSpec((dr, tf), lambda i, j: (i, fblk(j))) if w.shape == (d, f)
                else pl.BlockSpec((tf, dr), lambda i, j: (fblk(j), i)))
        in_specs.append(spec)
        out_specs.append(spec)
        out_shape.append(jax.ShapeDtypeStruct(w.shape, BF16))
    outs = pl.pallas_call(
        functools.partial(_ffn_kernel, emit_hidden=emit_hidden, n_cast=len(cast_along), n_io=n_io, n_f=n_f),
        grid=grid, in_specs=in_specs, out_specs=out_specs, out_shape=out_shape,
        scratch_shapes=[pltpu.VMEM((bm, d), BF16), pltpu.VMEM((bm, d), F32)],
        compiler_params=_cparams(("parallel", "arbitrary"), 60),
        name="ffn_hidden" if emit_hidden else "ffn_final",
    )(x, g_in.reshape(1, d), wg, wu, wd, g_out.reshape(1, d), *cast_along)
    return outs if len(outs) > 1 else outs[0]


def _rope(y, cos, sin_signed, first_half):
    partner = jnp.where(first_half, pltpu.roll(y, HEAD_DIM - 32, axis=1), pltpu.roll(y, 32, axis=1))
    return y * cos + partner * sin_signed


def _head_pairs(u_ref, w_ref, width):
    u = u_ref[...]
    for c in range(0, width, 2 * HEAD_DIM):
        yield c // HEAD_DIM, _dot(u, w_ref[:, c:c + 2 * HEAD_DIM])


def _proj_q_kernel(u_ref, w_ref, cos_ref, sin_ref, gain_ref, q_ref, acc_ref):
    @pl.when(pl.program_id(0) == 0)
    def _():
        acc_ref[...] = jnp.zeros(acc_ref.shape, F32)

    cos, sin = cos_ref[...], sin_ref[...]
    first_half = (lax.broadcasted_iota(jnp.int32, cos.shape, 1) % 64) < 32
    scale = HEAD_DIM ** -0.5
    for h in range(q_ref.shape[1] // HEAD_DIM):
        lanes = slice(h * HEAD_DIM, (h + 1) * HEAD_DIM)
        y = _rope(_rms(acc_ref[:, lanes], gain_ref[...]), cos, sin, first_half)
        q_ref[:, lanes] = (y * scale).astype(BF16)
    acc_ref[...] = _dot(u_ref[...], w_ref[...])


def _proj_kv_kernel(u_ref, w_ref, cos_ref, sin_ref, gain_ref, kt_ref, v_ref):
    cos, sin = cos_ref[...], sin_ref[...]
    first_half = (lax.broadcasted_iota(jnp.int32, cos.shape, 1) % 64) < 32
    for h0, acc in _head_pairs(u_ref, w_ref, 2 * A_KV_W):
        for h in range(2):
            part = acc[:, h * HEAD_DIM:(h + 1) * HEAD_DIM]
            if h0 + h < A_KV_HEADS:
                y = _rope(_rms(part, gain_ref[...]), cos, sin, first_half)
                kt_ref[0, (h0 + h) * HEAD_DIM:(h0 + h + 1) * HEAD_DIM, :] = y.T.astype(BF16)
            else:
                hv = h0 + h - A_KV_HEADS
                v_ref[:, hv * HEAD_DIM:(hv + 1) * HEAD_DIM] = part.astype(BF16)


def _rope_tables(t):
    rows = t // GRID_W
    row_ids = jnp.repeat(jnp.arange(rows), GRID_W).astype(F32)
    col_ids = jnp.tile(jnp.arange(GRID_W), rows).astype(F32)
    inv = ROPE_THETA ** (-jnp.arange(0, ROPE_AXIS_DIM, 2, dtype=F32) / ROPE_AXIS_DIM)
    ang_r = row_ids[:, None] * inv[None, :]
    ang_c = col_ids[:, None] * inv[None, :]
    cos = jnp.concatenate([jnp.cos(ang_r)] * 2 + [jnp.cos(ang_c)] * 2, axis=-1)
    sin = jnp.concatenate([-jnp.sin(ang_r), jnp.sin(ang_r), -jnp.sin(ang_c), jnp.sin(ang_c)], axis=-1)
    return cos, sin


def _proj_a(u, w, q_gain, k_gain, *, bsz, bm):
    n, d = u.shape
    t = n // bsz
    assert t % bm == 0 and A_Q_W % PROJ_BN == 0
    tiles_per_seq = t // bm
    cos, sin = _rope_tables(t)
    u_spec = pl.BlockSpec((bm, d), lambda i, j: (i, 0))
    tab = pl.BlockSpec((bm, HEAD_DIM), lambda i, j: (i % tiles_per_seq, 0))
    vec = pl.BlockSpec((1, HEAD_DIM), lambda i, j: (0, 0))
    nq = A_Q_W // PROJ_BN
    steps = (n // bm) * nq
    cur = lambda s: jnp.minimum(s, steps - 1)
    fin = lambda s: jnp.maximum(s - 1, 0)
    fin_tab = pl.BlockSpec((bm, HEAD_DIM), lambda s: ((fin(s) // nq) % tiles_per_seq, 0))
    qp = pl.pallas_call(
        _proj_q_kernel,
        grid=(steps + 1,),
        in_specs=[pl.BlockSpec((bm, d), lambda s: (cur(s) // nq, 0)),
                  pl.BlockSpec((d, PROJ_BN), lambda s: (0, cur(s) % nq)),
                  fin_tab, fin_tab, pl.BlockSpec((1, HEAD_DIM), lambda s: (0, 0))],
        out_specs=pl.BlockSpec((bm, PROJ_BN), lambda s: (fin(s) // nq, fin(s) % nq)),
        out_shape=jax.ShapeDtypeStruct((n, A_Q_W), BF16),
        scratch_shapes=[pltpu.VMEM((bm, PROJ_BN), F32)],
        compiler_params=_cparams(("arbitrary",), 56),
        name="proj_q",
    )(u, w, cos, sin, q_gain.reshape(1, HEAD_DIM))
    kt, v = pl.pallas_call(
        _proj_kv_kernel,
        grid=(n // bm, 1),
        in_specs=[u_spec, pl.BlockSpec((d, 2 * A_KV_W), lambda i, j: (0, A_Q_W // (2 * A_KV_W))), tab, tab, vec],
        out_specs=(pl.BlockSpec((1, A_KV_W, bm), lambda i, j: (i // tiles_per_seq, 0, i % tiles_per_seq)),
                   pl.BlockSpec((bm, A_KV_W), lambda i, j: (i, 0))),
        out_shape=(jax.ShapeDtypeStruct((bsz, A_KV_W, t), BF16), jax.ShapeDtypeStruct((n, A_KV_W), BF16)),
        compiler_params=_cparams(("parallel", "arbitrary"), 48),
        name="proj_kv",
    )(u, w, cos, sin, k_gain.reshape(1, HEAD_DIM))
    return qp, kt, v


def _proj_cols_kernel(u_ref, w_ref, o_ref, *, sigmoid):
    acc = _dot(u_ref[...], w_ref[...])
    if sigmoid:
        acc = jax.nn.sigmoid(acc)
    o_ref[...] = acc.astype(o_ref.dtype)


def _proj_cols(u, w, *, col0, width, bm, sigmoid, name):
    n, d = u.shape
    bn = _tile(width, PROJ_BN)
    assert col0 % bn == 0
    return pl.pallas_call(
        functools.partial(_proj_cols_kernel, sigmoid=sigmoid),
        grid=(n // bm, width // bn),
        in_specs=[pl.BlockSpec((bm, d), lambda i, j: (i, 0)),
                  pl.BlockSpec((d, bn), lambda i, j: (0, col0 // bn + j))],
        out_specs=pl.BlockSpec((bm, bn), lambda i, j: (i, j)),
        out_shape=jax.ShapeDtypeStruct((n, width), BF16),
        compiler_params=_cparams(("parallel", "arbitrary"), 48),
        name=name,
    )(u, w)


def _proj_cls_kernel(u_ref, w_ref, o_ref, acc_ref, *, dilation):
    acc = _dot(u_ref[...], w_ref[...])
    rows = o_ref.shape[1]
    for c in range(acc_ref.shape[0]):
        lanes = slice(c * HEAD_DIM, (c + 1) * HEAD_DIM)
        acc_ref[c] = acc[:, lanes]
        for r in range(dilation):
            o_ref[r, :, lanes] = acc_ref[c, pl.ds(r, rows, stride=dilation), :].astype(o_ref.dtype)


def _proj_cls(u, w, *, col0, width, bm, dilation, name):
    n, d = u.shape
    assert col0 % PROJ_BN == 0
    return pl.pallas_call(
        functools.partial(_proj_cls_kernel, dilation=dilation),
        grid=(n // bm, width // PROJ_BN),
        in_specs=[pl.BlockSpec((bm, d), lambda i, j: (i, 0)),
                  pl.BlockSpec((d, PROJ_BN), lambda i, j: (0, col0 // PROJ_BN + j))],
        out_specs=pl.BlockSpec((dilation, bm // dilation, PROJ_BN), lambda i, j: (0, i, j)),
        out_shape=jax.ShapeDtypeStruct((dilation, n // dilation, width), BF16),
        scratch_shapes=[pltpu.VMEM((PROJ_BN // HEAD_DIM, bm, HEAD_DIM), F32)],
        compiler_params=_cparams(("parallel", "arbitrary"), 56),
        name=name,
    )(u, w)


def _attn_a_kernel(q_ref, kt_ref, v_ref, o_ref):
    kt = kt_ref[0]
    v = v_ref[0]
    for r0 in range(0, q_ref.shape[1], ATTN_A_ROWS):
        rows = slice(r0, r0 + ATTN_A_ROWS)
        for g in range(A_GROUP):
            sl = slice(g * HEAD_DIM, (g + 1) * HEAD_DIM)
            s = _dot(q_ref[0, rows, sl], kt)
            p = jnp.exp(s - jnp.max(s, axis=-1, keepdims=True))
            denom = jnp.sum(p, axis=-1, keepdims=True)
            o = _dot(p.astype(BF16), v) / denom
            o_ref[0, rows, sl] = o.astype(o_ref.dtype)


def _attn_a(qp, kt, v, *, tq):
    b, t, _ = qp.shape
    gw = A_GROUP * HEAD_DIM
    assert tq % ATTN_A_ROWS == 0
    return pl.pallas_call(
        _attn_a_kernel,
        grid=(b, A_KV_HEADS, t // tq),
        in_specs=[pl.BlockSpec((1, tq, gw), lambda bi, kv, i: (bi, i, kv)),
                  pl.BlockSpec((1, HEAD_DIM, t), lambda bi, kv, i: (bi, kv, 0)),
                  pl.BlockSpec((1, t, HEAD_DIM), lambda bi, kv, i: (bi, 0, kv))],
        out_specs=pl.BlockSpec((1, tq, gw), lambda bi, kv, i: (bi, i, kv)),
        out_shape=jax.ShapeDtypeStruct((b, t, A_Q_W), BF16),
        compiler_params=_cparams(("parallel", "parallel", "arbitrary"), 48),
        name="attn_a",
    )(qp, kt, v)


def _attn_b_kernel(slopes_ref, q_ref, k_ref, v_ref, o_ref, lse_ref, *, group, dilation, half, tq, kw,
                   heads, interleave):
    length = q_ref.shape[0]
    head0 = group * B_HEADS_PER_GROUP + pl.program_id(1) * heads
    scale = HEAD_DIM ** -0.5
    row = lax.broadcasted_iota(jnp.int32, (tq, kw), 0)
    col = lax.broadcasted_iota(jnp.int32, (tq, kw), 1)

    def tile(i, lanes, slope):
        m0 = pl.multiple_of(i * tq, tq)
        ks = pl.multiple_of(jnp.clip(m0 - half, 0, length - kw), half)
        q = q_ref[pl.ds(m0, tq), lanes]
        k = k_ref[pl.ds(ks, kw), lanes]
        v = v_ref[pl.ds(ks, kw), lanes]
        s = lax.dot_general(q, k, (((1,), (1,)), ((), ())), preferred_element_type=F32) * scale
        dist = jnp.abs(col - row + (ks - m0))
        bias = -slope * (dist * dilation).astype(F32)
        s = jnp.where(dist <= half, s + bias, NEG_INF)
        m = jnp.max(s, axis=-1, keepdims=True)
        p = jnp.exp(s - m)
        denom = jnp.sum(p, axis=-1, keepdims=True)
        o = _dot(p.astype(BF16), v) / denom
        o_ref[pl.ds(m0, tq), lanes] = o.astype(o_ref.dtype)
        lse_ref[pl.ds(m0, tq), lanes] = jnp.broadcast_to(m + jnp.log(denom), (tq, HEAD_DIM))

    for hh in range(heads):
        lanes = slice(hh * HEAD_DIM, (hh + 1) * HEAD_DIM)
        slope = slopes_ref[head0 + hh]

        def body(it, carry, lanes=lanes, slope=slope):
            for u in range(interleave):
                tile(it * interleave + u, lanes, slope)
            return carry

        lax.fori_loop(0, length // (tq * interleave), body, 0)


def _attn_b(qkv, slopes, group, *, bsz):
    window, dilation = B_PATTERNS[group]
    length = qkv.shape[-2] // (1 if dilation == 1 else bsz)
    half = (window // 2) // dilation
    tq = min(128, length)
    kw = min(length, tq + 2 * half)
    heads = 1 if dilation == 1 else B_HEADS_PER_GROUP
    steps = B_HEADS_PER_GROUP // heads
    width = heads * HEAD_DIM

    if dilation == 1:
        in_spec = lambda part: pl.BlockSpec((None, length, width), lambda bi, h, r, s: (bi, 0, part * steps + h))
        out_spec = pl.BlockSpec((None, length, width), lambda bi, h, r, s: (bi, 0, h))
        out_dims = (bsz, length, B_OUT_W)
    else:
        in_spec = lambda part: pl.BlockSpec((None, length, width), lambda bi, h, r, s: (r, bi, part * steps + h))
        out_spec = pl.BlockSpec((None, length, width), lambda bi, h, r, s: (r, bi, h))
        out_dims = (dilation, bsz * length, B_OUT_W)
    n_tiles = length // tq
    return pl.pallas_call(
        functools.partial(_attn_b_kernel, group=group, dilation=dilation, half=half, tq=tq, kw=kw,
                          heads=heads, interleave=min(4, n_tiles)),
        grid_spec=pltpu.PrefetchScalarGridSpec(
            num_scalar_prefetch=1,
            grid=(bsz, steps, dilation),
            in_specs=[in_spec(0), in_spec(1), in_spec(2)],
            out_specs=(out_spec, out_spec)),
        out_shape=(jax.ShapeDtypeStruct(out_dims, BF16), jax.ShapeDtypeStruct(out_dims, F32)),
        compiler_params=_cparams(("parallel", "parallel", "parallel"), 40),
        name=f"attn_b{group}",
    )(slopes, qkv, qkv, qkv)


def _merge_b_kernel(o0_ref, l0_ref, o1_ref, l1_ref, o2_ref, l2_ref, yb_ref, on1_ref, ln1_ref, on2_ref, ln2_ref):
    for o_ref, l_ref, on_ref, ln_ref in ((o1_ref, l1_ref, on1_ref, ln1_ref),
                                         (o2_ref, l2_ref, on2_ref, ln2_ref)):
        dilation, rows = o_ref.shape[0], o_ref.shape[1]
        for c in range(on_ref.shape[0]):
            lanes = slice(c * HEAD_DIM, (c + 1) * HEAD_DIM)
            for r in range(dilation):
                on_ref[c, pl.ds(r, rows, stride=dilation), :] = o_ref[r, :, lanes].astype(F32)
                ln_ref[c, pl.ds(r, rows, stride=dilation), :] = l_ref[r, :, lanes]
    for c in range(on1_ref.shape[0]):
        lanes = slice(c * HEAD_DIM, (c + 1) * HEAD_DIM)
        l0, l1, l2 = l0_ref[:, lanes], ln1_ref[c], ln2_ref[c]
        m = jnp.maximum(jnp.maximum(l0, l1), l2)
        e0, e1, e2 = jnp.exp(l0 - m), jnp.exp(l1 - m), jnp.exp(l2 - m)
        tot = e0 + e1 + e2
        yb = (e0 / tot) * o0_ref[:, lanes].astype(F32) + (e1 / tot) * on1_ref[c] + (e2 / tot) * on2_ref[c]
        yb_ref[:, lanes] = yb.astype(BF16)


def _merge_b(o_l, *, bm):
    (o0, l0), (o1, l1), (o2, l2) = o_l
    n = o0.shape[0]
    row = pl.BlockSpec((bm, B_OUT_W), lambda i: (i, 0))

    def cls(arr):
        dil = arr.shape[0]
        return pl.BlockSpec((dil, bm // dil, B_OUT_W), lambda i: (0, i, 0))

    return pl.pallas_call(
        _merge_b_kernel,
        grid=(n // bm,),
        in_specs=[row, row, cls(o1), cls(l1), cls(o2), cls(l2)],
        out_specs=row,
        out_shape=jax.ShapeDtypeStruct((n, B_OUT_W), BF16),
        scratch_shapes=[pltpu.VMEM((B_HEADS_PER_GROUP, bm, HEAD_DIM), F32)] * 4,
        compiler_params=_cparams(("parallel",), 40),
        name="merge_b",
    )(o0, l0, o1, l1, o2, l2)


def _branch_kernel(ya_ref, yb_ref, wa_ref, wb_ref, ga_ref, gb_ref, out_ref):
    a = _dot(ya_ref[...], wa_ref[...])
    bb = _dot(yb_ref[...], wb_ref[...])
    merged = ga_ref[...].astype(F32) * a + gb_ref[...].astype(F32) * bb
    out_ref[...] = merged.astype(out_ref.dtype)


def _branch(ya, yb, wa, wb, gates, *, bm, bn):
    n, d = ya.shape[0], wa.shape[1]
    row = lambda w: pl.BlockSpec((bm, w), lambda i, j: (i, 0))
    return pl.pallas_call(
        _branch_kernel,
        grid=(n // bm, d // bn),
        in_specs=[row(A_Q_W), row(B_OUT_W),
                  pl.BlockSpec((A_Q_W, bn), lambda i, j: (0, j)),
                  pl.BlockSpec((B_OUT_W, bn), lambda i, j: (0, j)),
                  pl.BlockSpec((bm, bn), lambda i, j: (i, j)),
                  pl.BlockSpec((bm, bn), lambda i, j: (i, d // bn + j))],
        out_specs=pl.BlockSpec((bm, bn), lambda i, j: (i, j)),
        out_shape=jax.ShapeDtypeStruct((n, d), BF16),
        compiler_params=_cparams(("parallel", "arbitrary"), 48),
        name="branch_proj",
    )(ya, yb, wa, wb, gates, gates)


def _out_kernel(h_ref, a_ref, w_ref, o_ref):
    o_ref[...] = h_ref[...] + _dot(a_ref[...], w_ref[...])


def _out_proj(h, a, w, *, bm, bn):
    n, d = h.shape
    k = a.shape[1]
    return pl.pallas_call(
        _out_kernel,
        grid=(n // bm, d // bn),
        in_specs=[pl.BlockSpec((bm, bn), lambda i, j: (i, j)),
                  pl.BlockSpec((bm, k), lambda i, j: (i, 0)),
                  pl.BlockSpec((k, bn), lambda i, j: (0, j))],
        out_specs=pl.BlockSpec((bm, bn), lambda i, j: (i, j)),
        out_shape=jax.ShapeDtypeStruct((n, d), F32),
        compiler_params=_cparams(("parallel", "arbitrary"), 56),
        name="out_proj",
    )(h, a, w)


def _layer(h, g_ffn1, w1_gate, w1_up, w1_down, g_mix, w_in, q_norm_a, k_norm_a,
           w_branch_a, w_branch_b, w_out, g_ffn2, w2_gate, w2_up, w2_down, g_next, *, bsz, last):
    n, d = h.shape
    t = n // bsz
    c = lambda w: w.astype(BF16)
    bm = _tile(n, 512)
    ffn = functools.partial(_ffn, bm=_tile(n, 1024), tf=_tile(w1_gate.shape[1], 256), io_rows=_tile(n, 128))

    ffn2_w = (w2_gate, w2_up, w2_down)
    if w2_gate.shape == w1_gate.shape and _can_cast_along(n, d, w1_gate.shape[1], ffn.keywords["bm"],
                                                          ffn.keywords["tf"]):
        h, u, *ffn2_w = ffn(h, g_ffn1, c(w1_gate), c(w1_up), c(w1_down), g_mix, emit_hidden=True,
                            cast_along=ffn2_w)
    else:
        h, u = ffn(h, g_ffn1, c(w1_gate), c(w1_up), c(w1_down), g_mix, emit_hidden=True)
        ffn2_w = [c(w) for w in ffn2_w]

    pm = _tile(t, 1024)
    a_w = A_Q_W + 2 * A_KV_W
    w_in = c(w_in)
    qp, kt, va = _proj_a(u, w_in, q_norm_a, k_norm_a, bsz=bsz, bm=pm)
    qkv_b = [_proj_cols(u, w_in, col0=a_w, width=B_GROUP_W, bm=pm, sigmoid=False,
                        name="proj_b0").reshape(bsz, t, B_GROUP_W)]
    for g in range(1, B_N_GROUPS):
        qkv_b.append(_proj_cls(u, w_in, col0=a_w + g * B_GROUP_W, width=B_GROUP_W, bm=pm,
                               dilation=B_PATTERNS[g][1], name=f"proj_b{g}"))
    gates = _proj_cols(u, w_in, col0=a_w + B_QKV_W, width=2 * d, bm=pm, sigmoid=True, name="proj_gates")

    ya = _attn_a(qp.reshape(bsz, t, A_Q_W), kt, va.reshape(bsz, t, A_KV_W), tq=_tile(t, 1024)).reshape(n, A_Q_W)

    slopes = jnp.exp2(-8.0 * jnp.arange(1, B_HEADS + 1, dtype=F32) / B_HEADS)
    o_l = [_attn_b(qkv_b[g], slopes, g, bsz=bsz) for g in range(B_N_GROUPS)]
    o_l[0] = tuple(a.reshape(n, B_OUT_W) for a in o_l[0])

    yb = _merge_b(o_l, bm=bm)
    merged = _branch(ya, yb, c(w_branch_a), c(w_branch_b), gates, bm=_tile(n, 1024), bn=_tile(d, 1024))
    h = _out_proj(h, merged, c(w_out), bm=_tile(n, 1024), bn=_tile(d, 1024))

    if last:
        return ffn(h, g_ffn2, *ffn2_w, g_next, emit_hidden=False, io_rows=_tile(n, 256))
    return ffn(h, g_ffn2, *ffn2_w, g_next, emit_hidden=True)[0]


def kernel(x, g_ffn1, w1_gate, w1_up, w1_down, g_mix, w_in, q_norm_a, k_norm_a, w_branch_a, w_branch_b,
           w_out, g_ffn2, w2_gate, w2_up, w2_down, g_final):
    bsz, t, d = x.shape
    depth = g_ffn1.shape[0]
    h = x.reshape(bsz * t, d)
    for l in range(depth):
        last = l == depth - 1
        h = _layer(h, g_ffn1[l], w1_gate[l], w1_up[l], w1_down[l], g_mix[l], w_in[l], q_norm_a[l],
                   k_norm_a[l], w_branch_a[l], w_branch_b[l], w_out[l], g_ffn2[l], w2_gate[l], w2_up[l],
                   w2_down[l], g_final if last else g_ffn1[l + 1], bsz=bsz, last=last)
    return h.reshape(bsz, t, d)
```

```python
import functools

import jax
import jax.numpy as jnp
from jax import lax
from jax.experimental import pallas as pl
from jax.experimental.pallas import tpu as pltpu

HEAD_DIM = 128
A_HEADS = 16
A_KV_HEADS = 4
A_GROUP = A_HEADS // A_KV_HEADS
B_PATTERNS = ((128, 1), (512, 4), (2048, 16))
B_HEADS_PER_GROUP = 8
B_N_GROUPS = len(B_PATTERNS)
B_HEADS = B_N_GROUPS * B_HEADS_PER_GROUP
A_Q_W = A_HEADS * HEAD_DIM
A_KV_W = A_KV_HEADS * HEAD_DIM
B_OUT_W = B_HEADS_PER_GROUP * HEAD_DIM
B_GROUP_W = 3 * B_OUT_W
B_QKV_W = B_N_GROUPS * B_GROUP_W
GRID_W = 64
ROPE_THETA = 10000.0
ROPE_AXIS_DIM = HEAD_DIM // 2
RMS_EPS = 1e-6
NEG_INF = -1e30

DOWN_CHUNK = 512
FFN_MM_ROWS = 512
ATTN_A_ROWS = 256
PROJ_BN = 2 * A_KV_W
LANES = 128
BF16_SUBLANES = 16
MIB = 1024 * 1024
BF16 = jnp.bfloat16
F32 = jnp.float32


def _cparams(sem, vmem_mib):
    return pltpu.CompilerParams(dimension_semantics=sem, vmem_limit_bytes=vmem_mib * MIB)


def _tile(n, pref):
    t = min(n, pref)
    while n % t:
        t //= 2
    return t


def _rms(x, gain):
    y = x * lax.rsqrt(jnp.mean(x * x, axis=-1, keepdims=True) + RMS_EPS)
    return y * gain


def _dot(a, b):
    return jnp.dot(a, b, preferred_element_type=F32)


def _ffn_kernel(x_ref, gin_ref, wg_ref, wu_ref, wd_ref, gout_ref, *rest, emit_hidden, n_cast, n_io, n_f):
    cast_in, rest = rest[:n_cast], rest[n_cast:]
    outs, rest = rest[:2 if emit_hidden else 1], rest[2 if emit_hidden else 1:]
    cast_out, (xn_ref, acc_ref) = rest[:n_cast], rest[n_cast:]
    y_ref = outs[-1]
    io_rows = x_ref.shape[0]
    j = pl.program_id(1)
    first_mm, last_mm = n_io - 1, n_io + n_f - 2

    for src_ref, dst_ref in zip(cast_in, cast_out):
        dst_ref[...] = src_ref[...].astype(dst_ref.dtype)

    @pl.when(j <= first_mm)
    def _():
        rows = pl.ds(pl.multiple_of(j * io_rows, io_rows), io_rows)
        x = x_ref[...]
        xn_ref[rows, :] = _rms(x, gin_ref[...]).astype(BF16)
        acc_ref[rows, :] = x

    @pl.when((j >= first_mm) & (j <= last_mm))
    def _():
        for r0 in range(0, acc_ref.shape[0], FFN_MM_ROWS):
            rows = slice(r0, r0 + FFN_MM_ROWS)
            xn = xn_ref[rows, :]
            gate = _dot(xn, wg_ref[...])
            up = _dot(xn, wu_ref[...])
            act = (jax.nn.silu(gate) * up * 0.5).astype(BF16)
            for c in range(0, acc_ref.shape[1], DOWN_CHUNK):
                acc_ref[rows, c:c + DOWN_CHUNK] += _dot(act, wd_ref[:, c:c + DOWN_CHUNK])

    @pl.when(j >= last_mm)
    def _():
        rows = pl.ds(pl.multiple_of((j - last_mm) * io_rows, io_rows), io_rows)
        h = acc_ref[rows, :]
        if emit_hidden:
            outs[0][...] = h
        y_ref[...] = _rms(h, gout_ref[...]).astype(y_ref.dtype)


def _cast_block(shape, row_tiles, col_steps):
    rows, cols = shape
    if rows % row_tiles or (rows // row_tiles) % BF16_SUBLANES:
        return None
    for width in range(LANES, cols + 1, LANES):
        if cols % width == 0 and cols // width <= col_steps:
            return rows // row_tiles, width
    return None


def _ffn(x, g_in, wg, wu, wd, g_out, *, emit_hidden, bm, tf, io_rows, cast_along=()):
    n, d = x.shape
    f = wg.shape[1]
    n_io, n_f = bm // io_rows, f // tf
    assert bm % FFN_MM_ROWS == 0 and bm % io_rows == 0
    grid = (n // bm, n_f + 2 * (n_io - 1))
    fblk = lambda j: jnp.clip(j - (n_io - 1), 0, n_f - 1)
    vec = pl.BlockSpec((1, d), lambda i, j: (0, 0))
    in_specs = [pl.BlockSpec((io_rows, d), lambda i, j: (i * n_io + jnp.minimum(j, n_io - 1), 0)), vec,
                pl.BlockSpec((d, tf), lambda i, j: (0, fblk(j))),
                pl.BlockSpec((d, tf), lambda i, j: (0, fblk(j))),
                pl.BlockSpec((tf, d), lambda i, j: (fblk(j), 0)),
                vec]
    out_row = pl.BlockSpec((io_rows, d), lambda i, j: (i * n_io + jnp.clip(j - (n_io + n_f - 2), 0, n_io - 1), 0))
    if emit_hidden:
        out_shape = [jax.ShapeDtypeStruct((n, d), F32), jax.ShapeDtypeStruct((n, d), BF16)]
        out_specs = [out_row, out_row]
    else:
        out_shape = [jax.ShapeDtypeStruct((n, d), F32)]
        out_specs = [out_row]
    for w in cast_along:
        blk = _cast_block(w.shape, grid[0], n_f)
        last = w.shape[1] // blk[1] - 1
        spec = pl.BlockSpec(blk, lambda i, j, last=last: (i, jnp.minimum(fblk(j), last)))
        in_specs.append(spec)
        out_specs.append(spec)
        out_shape.append(jax.ShapeDtypeStruct(w.shape, BF16))
    outs = pl.pallas_call(
        functools.partial(_ffn_kernel, emit_hidden=emit_hidden, n_cast=len(cast_along), n_io=n_io, n_f=n_f),
        grid=grid, in_specs=in_specs, out_specs=out_specs, out_shape=out_shape,
        scratch_shapes=[pltpu.VMEM((bm, d), BF16), pltpu.VMEM((bm, d), F32)],
        compiler_params=_cparams(("parallel", "arbitrary"), 60),
        name="ffn_hidden" if emit_hidden else "ffn_final",
    )(x, g_in.reshape(1, d), wg, wu, wd, g_out.reshape(1, d), *cast_along)
    return outs if len(outs) > 1 else outs[0]


def _rope(y, cos, sin_signed, first_half):
    partner = jnp.where(first_half, pltpu.roll(y, HEAD_DIM - 32, axis=1), pltpu.roll(y, 32, axis=1))
    return y * cos + partner * sin_signed


def _head_pairs(u_ref, w_ref, width):
    u = u_ref[...]
    for c in range(0, width, 2 * HEAD_DIM):
        yield c // HEAD_DIM, _dot(u, w_ref[:, c:c + 2 * HEAD_DIM])


def _proj_q_kernel(u_ref, w_ref, cos_ref, sin_ref, gain_ref, q_ref, acc_ref):
    @pl.when(pl.program_id(0) == 0)
    def _():
        acc_ref[...] = jnp.zeros(acc_ref.shape, F32)

    cos, sin = cos_ref[...], sin_ref[...]
    first_half = (lax.broadcasted_iota(jnp.int32, cos.shape, 1) % 64) < 32
    scale = HEAD_DIM ** -0.5
    for h in range(q_ref.shape[1] // HEAD_DIM):
        lanes = slice(h * HEAD_DIM, (h + 1) * HEAD_DIM)
        y = _rope(_rms(acc_ref[:, lanes], gain_ref[...]), cos, sin, first_half)
        q_ref[:, lanes] = (y * scale).astype(BF16)
    acc_ref[...] = _dot(u_ref[...], w_ref[...])


def _proj_kv_kernel(u_ref, w_ref, cos_ref, sin_ref, gain_ref, kt_ref, v_ref):
    cos, sin = cos_ref[...], sin_ref[...]
    first_half = (lax.broadcasted_iota(jnp.int32, cos.shape, 1) % 64) < 32
    for h0, acc in _head_pairs(u_ref, w_ref, 2 * A_KV_W):
        for h in range(2):
            part = acc[:, h * HEAD_DIM:(h + 1) * HEAD_DIM]
            if h0 + h < A_KV_HEADS:
                y = _rope(_rms(part, gain_ref[...]), cos, sin, first_half)
                kt_ref[0, (h0 + h) * HEAD_DIM:(h0 + h + 1) * HEAD_DIM, :] = y.T.astype(BF16)
            else:
                hv = h0 + h - A_KV_HEADS
                v_ref[:, hv * HEAD_DIM:(hv + 1) * HEAD_DIM] = part.astype(BF16)


def _rope_tables(t):
    rows = t // GRID_W
    row_ids = jnp.repeat(jnp.arange(rows), GRID_W).astype(F32)
    col_ids = jnp.tile(jnp.arange(GRID_W), rows).astype(F32)
    inv = ROPE_THETA ** (-jnp.arange(0, ROPE_AXIS_DIM, 2, dtype=F32) / ROPE_AXIS_DIM)
    ang_r = row_ids[:, None] * inv[None, :]
    ang_c = col_ids[:, None] * inv[None, :]
    cos = jnp.concatenate([jnp.cos(ang_r)] * 2 + [jnp.cos(ang_c)] * 2, axis=-1)
    sin = jnp.concatenate([-jnp.sin(ang_r), jnp.sin(ang_r), -jnp.sin(ang_c), jnp.sin(ang_c)], axis=-1)
    return cos, sin


def _proj_a(u, w, q_gain, k_gain, *, bsz, bm):
    n, d = u.shape
    t = n // bsz
    assert t % bm == 0 and A_Q_W % PROJ_BN == 0
    tiles_per_seq = t // bm
    cos, sin = _rope_tables(t)
    u_spec = pl.BlockSpec((bm, d), lambda i, j: (i, 0))
    tab = pl.BlockSpec((bm, HEAD_DIM), lambda i, j: (i % tiles_per_seq, 0))
    vec = pl.BlockSpec((1, HEAD_DIM), lambda i, j: (0, 0))
    nq = A_Q_W // PROJ_BN
    steps = (n // bm) * nq
    cur = lambda s: jnp.minimum(s, steps - 1)
    fin = lambda s: jnp.maximum(s - 1, 0)
    fin_tab = pl.BlockSpec((bm, HEAD_DIM), lambda s: ((fin(s) // nq) % tiles_per_seq, 0))
    qp = pl.pallas_call(
        _proj_q_kernel,
        grid=(steps + 1,),
        in_specs=[pl.BlockSpec((bm, d), lambda s: (cur(s) // nq, 0)),
                  pl.BlockSpec((d, PROJ_BN), lambda s: (0, cur(s) % nq)),
                  fin_tab, fin_tab, pl.BlockSpec((1, HEAD_DIM), lambda s: (0, 0))],
        out_specs=pl.BlockSpec((bm, PROJ_BN), lambda s: (fin(s) // nq, fin(s) % nq)),
        out_shape=jax.ShapeDtypeStruct((n, A_Q_W), BF16),
        scratch_shapes=[pltpu.VMEM((bm, PROJ_BN), F32)],
        compiler_params=_cparams(("arbitrary",), 56),
        name="proj_q",
    )(u, w, cos, sin, q_gain.reshape(1, HEAD_DIM))
    kt, v = pl.pallas_call(
        _proj_kv_kernel,
        grid=(n // bm, 1),
        in_specs=[u_spec, pl.BlockSpec((d, 2 * A_KV_W), lambda i, j: (0, A_Q_W // (2 * A_KV_W))), tab, tab, vec],
        out_specs=(pl.BlockSpec((1, A_KV_W, bm), lambda i, j: (i // tiles_per_seq, 0, i % tiles_per_seq)),
                   pl.BlockSpec((bm, A_KV_W), lambda i, j: (i, 0))),
        out_shape=(jax.ShapeDtypeStruct((bsz, A_KV_W, t), BF16), jax.ShapeDtypeStruct((n, A_KV_W), BF16)),
        compiler_params=_cparams(("parallel", "arbitrary"), 48),
        name="proj_kv",
    )(u, w, cos, sin, k_gain.reshape(1, HEAD_DIM))
    return qp, kt, v


def _proj_cols_kernel(u_ref, w_ref, o_ref, *, sigmoid):
    acc = _dot(u_ref[...], w_ref[...])
    if sigmoid:
        acc = jax.nn.sigmoid(acc)
    o_ref[...] = acc.astype(o_ref.dtype)


def _proj_cols(u, w, *, col0, width, bm, sigmoid, name):
    n, d = u.shape
    bn = _tile(width, PROJ_BN)
    assert col0 % bn == 0
    return pl.pallas_call(
        functools.partial(_proj_cols_kernel, sigmoid=sigmoid),
        grid=(n // bm, width // bn),
        in_specs=[pl.BlockSpec((bm, d), lambda i, j: (i, 0)),
                  pl.BlockSpec((d, bn), lambda i, j: (0, col0 // bn + j))],
        out_specs=pl.BlockSpec((bm, bn), lambda i, j: (i, j)),
        out_shape=jax.ShapeDtypeStruct((n, width), BF16),
        compiler_params=_cparams(("parallel", "arbitrary"), 48),
        name=name,
    )(u, w)


def _proj_cls_kernel(u_ref, w_ref, o_ref, acc_ref, *, dilation):
    acc = _dot(u_ref[...], w_ref[...])
    rows = o_ref.shape[1]
    for c in range(acc_ref.shape[0]):
        lanes = slice(c * HEAD_DIM, (c + 1) * HEAD_DIM)
        acc_ref[c] = acc[:, lanes]
        for r in range(dilation):
            o_ref[r, :, lanes] = acc_ref[c, pl.ds(r, rows, stride=dilation), :].astype(o_ref.dtype)


def _proj_cls(u, w, *, col0, width, bm, dilation, name):
    n, d = u.shape
    assert col0 % PROJ_BN == 0
    return pl.pallas_call(
        functools.partial(_proj_cls_kernel, dilation=dilation),
        grid=(n // bm, width // PROJ_BN),
        in_specs=[pl.BlockSpec((bm, d), lambda i, j: (i, 0)),
                  pl.BlockSpec((d, PROJ_BN), lambda i, j: (0, col0 // PROJ_BN + j))],
        out_specs=pl.BlockSpec((dilation, bm // dilation, PROJ_BN), lambda i, j: (0, i, j)),
        out_shape=jax.ShapeDtypeStruct((dilation, n // dilation, width), BF16),
        scratch_shapes=[pltpu.VMEM((PROJ_BN // HEAD_DIM, bm, HEAD_DIM), F32)],
        compiler_params=_cparams(("parallel", "arbitrary"), 56),
        name=name,
    )(u, w)


def _attn_a_kernel(q_ref, kt_ref, v_ref, o_ref):
    kt = kt_ref[0]
    v = v_ref[0]
    for r0 in range(0, q_ref.shape[1], ATTN_A_ROWS):
        rows = slice(r0, r0 + ATTN_A_ROWS)
        for g in range(A_GROUP):
            sl = slice(g * HEAD_DIM, (g + 1) * HEAD_DIM)
            s = _dot(q_ref[0, rows, sl], kt)
            p = jnp.exp(s - jnp.max(s, axis=-1, keepdims=True))
            denom = jnp.sum(p, axis=-1, keepdims=True)
            o = _dot(p.astype(BF16), v) / denom
            o_ref[0, rows, sl] = o.astype(o_ref.dtype)


def _attn_a(qp, kt, v, *, tq):
    b, t, _ = qp.shape
    gw = A_GROUP * HEAD_DIM
    assert tq % ATTN_A_ROWS == 0
    return pl.pallas_call(
        _attn_a_kernel,
        grid=(b, A_KV_HEADS, t // tq),
        in_specs=[pl.BlockSpec((1, tq, gw), lambda bi, kv, i: (bi, i, kv)),
                  pl.BlockSpec((1, HEAD_DIM, t), lambda bi, kv, i: (bi, kv, 0)),
                  pl.BlockSpec((1, t, HEAD_DIM), lambda bi, kv, i: (bi, 0, kv))],
        out_specs=pl.BlockSpec((1, tq, gw), lambda bi, kv, i: (bi, i, kv)),
        out_shape=jax.ShapeDtypeStruct((b, t, A_Q_W), BF16),
        compiler_params=_cparams(("parallel", "parallel", "arbitrary"), 48),
        name="attn_a",
    )(qp, kt, v)


def _attn_b_kernel(slopes_ref, q_ref, k_ref, v_ref, o_ref, lse_ref, *, group, dilation, half, tq, kw,
                   heads, interleave):
    length = q_ref.shape[0]
    head0 = group * B_HEADS_PER_GROUP + pl.program_id(1) * heads
    scale = HEAD_DIM ** -0.5
    row = lax.broadcasted_iota(jnp.int32, (tq, kw), 0)
    col = lax.broadcasted_iota(jnp.int32, (tq, kw), 1)

    def tile(i, lanes, slope):
        m0 = pl.multiple_of(i * tq, tq)
        ks = pl.multiple_of(jnp.clip(m0 - half, 0, length - kw), half)
        q = q_ref[pl.ds(m0, tq), lanes]
        k = k_ref[pl.ds(ks, kw), lanes]
        v = v_ref[pl.ds(ks, kw), lanes]
        s = lax.dot_general(q, k, (((1,), (1,)), ((), ())), preferred_element_type=F32) * scale
        dist = jnp.abs(col - row + (ks - m0))
        bias = -slope * (dist * dilation).astype(F32)
        s = jnp.where(dist <= half, s + bias, NEG_INF)
        m = jnp.max(s, axis=-1, keepdims=True)
        p = jnp.exp(s - m)
        denom = jnp.sum(p, axis=-1, keepdims=True)
        o = _dot(p.astype(BF16), v) / denom
        o_ref[pl.ds(m0, tq), lanes] = o.astype(o_ref.dtype)
        lse_ref[pl.ds(m0, tq), lanes] = jnp.broadcast_to(m + jnp.log(denom), (tq, HEAD_DIM))

    for hh in range(heads):
        lanes = slice(hh * HEAD_DIM, (hh + 1) * HEAD_DIM)
        slope = slopes_ref[head0 + hh]

        def body(it, carry, lanes=lanes, slope=slope):
            for u in range(interleave):
                tile(it * interleave + u, lanes, slope)
            return carry

        lax.fori_loop(0, length // (tq * interleave), body, 0)


def _attn_b(qkv, slopes, group, *, bsz):
    window, dilation = B_PATTERNS[group]
    length = qkv.shape[-2] // (1 if dilation == 1 else bsz)
    half = (window // 2) // dilation
    tq = min(128, length)
    kw = min(length, tq + 2 * half)
    heads = 1 if dilation == 1 else B_HEADS_PER_GROUP
    steps = B_HEADS_PER_GROUP // heads
    width = heads * HEAD_DIM

    if dilation == 1:
        in_spec = lambda part: pl.BlockSpec((None, length, width), lambda bi, h, r, s: (bi, 0, part * steps + h))
        out_spec = pl.BlockSpec((None, length, width), lambda bi, h, r, s: (bi, 0, h))
        out_dims = (bsz, length, B_OUT_W)
    else:
        in_spec = lambda part: pl.BlockSpec((None, length, width), lambda bi, h, r, s: (r, bi, part * steps + h))
        out_spec = pl.BlockSpec((None, length, width), lambda bi, h, r, s: (r, bi, h))
        out_dims = (dilation, bsz * length, B_OUT_W)
    n_tiles = length // tq
    return pl.pallas_call(
        functools.partial(_attn_b_kernel, group=group, dilation=dilation, half=half, tq=tq, kw=kw,
                          heads=heads, interleave=min(8, n_tiles)),
        grid_spec=pltpu.PrefetchScalarGridSpec(
            num_scalar_prefetch=1,
            grid=(bsz, steps, dilation),
            in_specs=[in_spec(0), in_spec(1), in_spec(2)],
            out_specs=(out_spec, out_spec)),
        out_shape=(jax.ShapeDtypeStruct(out_dims, BF16), jax.ShapeDtypeStruct(out_dims, F32)),
        compiler_params=_cparams(("parallel", "parallel", "parallel"), 40),
        name=f"attn_b{group}",
    )(slopes, qkv, qkv, qkv)


def _merge_b_kernel(o0_ref, l0_ref, o1_ref, l1_ref, o2_ref, l2_ref, yb_ref, on1_ref, ln1_ref, on2_ref, ln2_ref):
    for o_ref, l_ref, on_ref, ln_ref in ((o1_ref, l1_ref, on1_ref, ln1_ref),
                                         (o2_ref, l2_ref, on2_ref, ln2_ref)):
        dilation, rows = o_ref.shape[0], o_ref.shape[1]
        for c in range(on_ref.shape[0]):
            lanes = slice(c * HEAD_DIM, (c + 1) * HEAD_DIM)
            for r in range(dilation):
                on_ref[c, pl.ds(r, rows, stride=dilation), :] = o_ref[r, :, lanes].astype(F32)
                ln_ref[c, pl.ds(r, rows, stride=dilation), :] = l_ref[r, :, lanes]
    for c in range(on1_ref.shape[0]):
        lanes = slice(c * HEAD_DIM, (c + 1) * HEAD_DIM)
        l0, l1, l2 = l0_ref[:, lanes], ln1_ref[c], ln2_ref[c]
        m = jnp.maximum(jnp.maximum(l0, l1), l2)
        e0, e1, e2 = jnp.exp(l0 - m), jnp.exp(l1 - m), jnp.exp(l2 - m)
        tot = e0 + e1 + e2
        yb = (e0 / tot) * o0_ref[:, lanes].astype(F32) + (e1 / tot) * on1_ref[c] + (e2 / tot) * on2_ref[c]
        yb_ref[:, lanes] = yb.astype(BF16)


def _merge_b(o_l, *, bm):
    (o0, l0), (o1, l1), (o2, l2) = o_l
    n = o0.shape[0]
    row = pl.BlockSpec((bm, B_OUT_W), lambda i: (i, 0))

    def cls(arr):
        dil = arr.shape[0]
        return pl.BlockSpec((dil, bm // dil, B_OUT_W), lambda i: (0, i, 0))

    return pl.pallas_call(
        _merge_b_kernel,
        grid=(n // bm,),
        in_specs=[row, row, cls(o1), cls(l1), cls(o2), cls(l2)],
        out_specs=row,
        out_shape=jax.ShapeDtypeStruct((n, B_OUT_W), BF16),
        scratch_shapes=[pltpu.VMEM((B_HEADS_PER_GROUP, bm, HEAD_DIM), F32)] * 4,
        compiler_params=_cparams(("parallel",), 40),
        name="merge_b",
    )(o0, l0, o1, l1, o2, l2)


def _branch_kernel(ya_ref, yb_ref, wa_ref, wb_ref, ga_ref, gb_ref, out_ref):
    a = _dot(ya_ref[...], wa_ref[...])
    bb = _dot(yb_ref[...], wb_ref[...])
    merged = ga_ref[...].astype(F32) * a + gb_ref[...].astype(F32) * bb
    out_ref[...] = merged.astype(out_ref.dtype)


def _branch(ya, yb, wa, wb, gates, *, bm, bn):
    n, d = ya.shape[0], wa.shape[1]
    row = lambda w: pl.BlockSpec((bm, w), lambda i, j: (i, 0))
    return pl.pallas_call(
        _branch_kernel,
        grid=(n // bm, d // bn),
        in_specs=[row(A_Q_W), row(B_OUT_W),
                  pl.BlockSpec((A_Q_W, bn), lambda i, j: (0, j)),
                  pl.BlockSpec((B_OUT_W, bn), lambda i, j: (0, j)),
                  pl.BlockSpec((bm, bn), lambda i, j: (i, j)),
                  pl.BlockSpec((bm, bn), lambda i, j: (i, d // bn + j))],
        out_specs=pl.BlockSpec((bm, bn), lambda i, j: (i, j)),
        out_shape=jax.ShapeDtypeStruct((n, d), BF16),
        compiler_params=_cparams(("parallel", "arbitrary"), 48),
        name="branch_proj",
    )(ya, yb, wa, wb, gates, gates)


def _out_kernel(h_ref, a_ref, w_ref, o_ref):
    o_ref[...] = h_ref[...] + _dot(a_ref[...], w_ref[...])


def _out_proj(h, a, w, *, bm, bn):
    n, d = h.shape
    k = a.shape[1]
    return pl.pallas_call(
        _out_kernel,
        grid=(n // bm, d // bn),
        in_specs=[pl.BlockSpec((bm, bn), lambda i, j: (i, j)),
                  pl.BlockSpec((bm, k), lambda i, j: (i, 0)),
                  pl.BlockSpec((k, bn), lambda i, j: (0, j))],
        out_specs=pl.BlockSpec((bm, bn), lambda i, j: (i, j)),
        out_shape=jax.ShapeDtypeStruct((n, d), F32),
        compiler_params=_cparams(("parallel", "arbitrary"), 56),
        name="out_proj",
    )(h, a, w)


def _layer(h, g_ffn1, w1_gate, w1_up, w1_down, g_mix, w_in, q_norm_a, k_norm_a,
           w_branch_a, w_branch_b, w_out, g_ffn2, w2_gate, w2_up, w2_down, g_next, *, bsz, last):
    n, d = h.shape
    t = n // bsz
    c = lambda w: w.astype(BF16)
    bm = _tile(n, 512)
    ffn = functools.partial(_ffn, bm=_tile(n, 1024), tf=_tile(w1_gate.shape[1], 256), io_rows=_tile(n, 128))

    later = [w2_gate, w2_up, w2_down, w_in, w_branch_a, w_branch_b, w_out]
    row_tiles, col_steps = n // ffn.keywords["bm"], w1_gate.shape[1] // ffn.keywords["tf"]
    rides = [_cast_block(w.shape, row_tiles, col_steps) is not None for w in later]
    h, u, *cast = ffn(h, g_ffn1, c(w1_gate), c(w1_up), c(w1_down), g_mix, emit_hidden=True,
                      cast_along=[w for w, ok in zip(later, rides) if ok])
    cast = iter(cast)
    *ffn2_w, w_in, w_branch_a, w_branch_b, w_out = [next(cast) if ok else c(w) for w, ok in zip(later, rides)]

    pm = _tile(t, 1024)
    a_w = A_Q_W + 2 * A_KV_W
    qp, kt, va = _proj_a(u, w_in, q_norm_a, k_norm_a, bsz=bsz, bm=pm)
    qkv_b = [_proj_cols(u, w_in, col0=a_w, width=B_GROUP_W, bm=pm, sigmoid=False,
                        name="proj_b0").reshape(bsz, t, B_GROUP_W)]
    for g in range(1, B_N_GROUPS):
        qkv_b.append(_proj_cls(u, w_in, col0=a_w + g * B_GROUP_W, width=B_GROUP_W, bm=pm,
                               dilation=B_PATTERNS[g][1], name=f"proj_b{g}"))
    gates = _proj_cols(u, w_in, col0=a_w + B_QKV_W, width=2 * d, bm=pm, sigmoid=True, name="proj_gates")

    ya = _attn_a(qp.reshape(bsz, t, A_Q_W), kt, va.reshape(bsz, t, A_KV_W), tq=_tile(t, 1024)).reshape(n, A_Q_W)

    slopes = jnp.exp2(-8.0 * jnp.arange(1, B_HEADS + 1, dtype=F32) / B_HEADS)
    o_l = [_attn_b(qkv_b[g], slopes, g, bsz=bsz) for g in range(B_N_GROUPS)]
    o_l[0] = tuple(a.reshape(n, B_OUT_W) for a in o_l[0])

    yb = _merge_b(o_l, bm=bm)
    merged = _branch(ya, yb, w_branch_a, w_branch_b, gates, bm=_tile(n, 1024), bn=_tile(d, 1024))
    h = _out_proj(h, merged, w_out, bm=_tile(n, 1024), bn=_tile(d, 1024))

    if last:
        return ffn(h, g_ffn2, *ffn2_w, g_next, emit_hidden=False, io_rows=_tile(n, 256))
    return ffn(h, g_ffn2, *ffn2_w, g_next, emit_hidden=True)[0]


def kernel(x, g_ffn1, w1_gate, w1_up, w1_down, g_mix, w_in, q_norm_a, k_norm_a, w_branch_a, w_branch_b,
           w_out, g_ffn2, w2_gate, w2_up, w2_down, g_final):
    bsz, t, d = x.shape
    depth = g_ffn1.shape[0]
    h = x.reshape(bsz * t, d)
    for l in range(depth):
        last = l == depth - 1
        h = _layer(h, g_ffn1[l], w1_gate[l], w1_up[l], w1_down[l], g_mix[l], w_in[l], q_norm_a[l],
                   k_norm_a[l], w_branch_a[l], w_branch_b[l], w_out[l], g_ffn2[l], w2_gate[l], w2_up[l],
                   w2_down[l], g_final if last else g_ffn1[l + 1], bsz=bsz, last=last)
    return h.reshape(bsz, t, d)
```

```python
import functools

import jax
import jax.numpy as jnp
from jax import lax
from jax.experimental import pallas as pl
from jax.experimental.pallas import tpu as pltpu

HEAD_DIM = 128
A_HEADS = 16
A_KV_HEADS = 4
A_GROUP = A_HEADS // A_KV_HEADS
B_PATTERNS = ((128, 1), (512, 4), (2048, 16))
B_HEADS_PER_GROUP = 8
B_N_GROUPS = len(B_PATTERNS)
B_HEADS = B_N_GROUPS * B_HEADS_PER_GROUP
A_Q_W = A_HEADS * HEAD_DIM
A_KV_W = A_KV_HEADS * HEAD_DIM
B_OUT_W = B_HEADS_PER_GROUP * HEAD_DIM
B_GROUP_W = 3 * B_OUT_W
B_QKV_W = B_N_GROUPS * B_GROUP_W
GRID_W = 64
ROPE_THETA = 10000.0
ROPE_AXIS_DIM = HEAD_DIM // 2
RMS_EPS = 1e-6
NEG_INF = -1e30

DOWN_CHUNK = 512
FFN_MM_ROWS = 512
ATTN_A_ROWS = 256
PROJ_BN = 2 * A_KV_W
LANES = 128
BF16_SUBLANES = 16
MIB = 1024 * 1024
BF16 = jnp.bfloat16
F32 = jnp.float32


def _cparams(sem, vmem_mib):
    return pltpu.CompilerParams(dimension_semantics=sem, vmem_limit_bytes=vmem_mib * MIB)


def _tile(n, pref):
    t = min(n, pref)
    while n % t:
        t //= 2
    return t


def _rms(x, gain):
    y = x * lax.rsqrt(jnp.mean(x * x, axis=-1, keepdims=True) + RMS_EPS)
    return y * gain


def _dot(a, b):
    return jnp.dot(a, b, preferred_element_type=F32)


def _ffn_kernel(x_ref, gin_ref, wg_ref, wu_ref, wd_ref, gout_ref, *rest, emit_hidden, n_cast, n_io, n_f):
    cast_in, rest = rest[:n_cast], rest[n_cast:]
    outs, rest = rest[:2 if emit_hidden else 1], rest[2 if emit_hidden else 1:]
    cast_out, (xn_ref, acc_ref) = rest[:n_cast], rest[n_cast:]
    y_ref = outs[-1]
    io_rows = x_ref.shape[0]
    j = pl.program_id(1)
    first_mm, last_mm = n_io - 1, n_io + n_f - 2

    for src_ref, dst_ref in zip(cast_in, cast_out):
        dst_ref[...] = src_ref[...].astype(dst_ref.dtype)

    @pl.when(j <= first_mm)
    def _():
        rows = pl.ds(pl.multiple_of(j * io_rows, io_rows), io_rows)
        x = x_ref[...]
        xn_ref[rows, :] = _rms(x, gin_ref[...]).astype(BF16)
        acc_ref[rows, :] = x

    @pl.when((j >= first_mm) & (j <= last_mm))
    def _():
        for r0 in range(0, acc_ref.shape[0], FFN_MM_ROWS):
            rows = slice(r0, r0 + FFN_MM_ROWS)
            xn = xn_ref[rows, :]
            gate = _dot(xn, wg_ref[...])
            up = _dot(xn, wu_ref[...])
            act = ((0.25 * gate) * (1.0 + jnp.tanh(0.5 * gate)) * up).astype(BF16)
            for c in range(0, acc_ref.shape[1], DOWN_CHUNK):
                acc_ref[rows, c:c + DOWN_CHUNK] += _dot(act, wd_ref[:, c:c + DOWN_CHUNK])

    @pl.when(j >= last_mm)
    def _():
        rows = pl.ds(pl.multiple_of((j - last_mm) * io_rows, io_rows), io_rows)
        h = acc_ref[rows, :]
        if emit_hidden:
            outs[0][...] = h
        y_ref[...] = _rms(h, gout_ref[...]).astype(y_ref.dtype)


def _cast_block(shape, row_tiles, col_steps):
    rows, cols = shape
    if rows % row_tiles or (rows // row_tiles) % BF16_SUBLANES:
        return None
    for width in range(LANES, cols + 1, LANES):
        if cols % width == 0 and cols // width <= col_steps:
            return rows // row_tiles, width
    return None


def _ffn(x, g_in, wg, wu, wd, g_out, *, emit_hidden, bm, tf, io_rows, cast_along=()):
    n, d = x.shape
    f = wg.shape[1]
    n_io, n_f = bm // io_rows, f // tf
    assert bm % FFN_MM_ROWS == 0 and bm % io_rows == 0
    grid = (n // bm, n_f + 2 * (n_io - 1))
    fblk = lambda j: jnp.clip(j - (n_io - 1), 0, n_f - 1)
    vec = pl.BlockSpec((1, d), lambda i, j: (0, 0))
    in_specs = [pl.BlockSpec((io_rows, d), lambda i, j: (i * n_io + jnp.minimum(j, n_io - 1), 0)), vec,
                pl.BlockSpec((d, tf), lambda i, j: (0, fblk(j))),
                pl.BlockSpec((d, tf), lambda i, j: (0, fblk(j))),
                pl.BlockSpec((tf, d), lambda i, j: (fblk(j), 0)),
                vec]
    out_row = pl.BlockSpec((io_rows, d), lambda i, j: (i * n_io + jnp.clip(j - (n_io + n_f - 2), 0, n_io - 1), 0))
    if emit_hidden:
        out_shape = [jax.ShapeDtypeStruct((n, d), F32), jax.ShapeDtypeStruct((n, d), BF16)]
        out_specs = [out_row, out_row]
    else:
        out_shape = [jax.ShapeDtypeStruct((n, d), F32)]
        out_specs = [out_row]
    for w in cast_along:
        blk = _cast_block(w.shape, grid[0], n_f)
        last = w.shape[1] // blk[1] - 1
        spec = pl.BlockSpec(blk, lambda i, j, last=last: (i, jnp.minimum(fblk(j), last)))
        in_specs.append(spec)
        out_specs.append(spec)
        out_shape.append(jax.ShapeDtypeStruct(w.shape, BF16))
    outs = pl.pallas_call(
        functools.partial(_ffn_kernel, emit_hidden=emit_hidden, n_cast=len(cast_along), n_io=n_io, n_f=n_f),
        grid=grid, in_specs=in_specs, out_specs=out_specs, out_shape=out_shape,
        scratch_shapes=[pltpu.VMEM((bm, d), BF16), pltpu.VMEM((bm, d), F32)],
        compiler_params=_cparams(("parallel", "arbitrary"), 60),
        name="ffn_hidden" if emit_hidden else "ffn_final",
    )(x, g_in.reshape(1, d), wg, wu, wd, g_out.reshape(1, d), *cast_along)
    return outs if len(outs) > 1 else outs[0]


def _rope(y, cos, sin_signed, first_half):
    partner = jnp.where(first_half, pltpu.roll(y, HEAD_DIM - 32, axis=1), pltpu.roll(y, 32, axis=1))
    return y * cos + partner * sin_signed


def _head_pairs(u_ref, w_ref, width):
    u = u_ref[...]
    for c in range(0, width, 2 * HEAD_DIM):
        yield c // HEAD_DIM, _dot(u, w_ref[:, c:c + 2 * HEAD_DIM])


def _proj_q_kernel(u_ref, w_ref, cos_ref, sin_ref, gain_ref, q_ref, acc_ref):
    @pl.when(pl.program_id(0) == 0)
    def _():
        acc_ref[...] = jnp.zeros(acc_ref.shape, F32)

    cos, sin = cos_ref[...], sin_ref[...]
    first_half = (lax.broadcasted_iota(jnp.int32, cos.shape, 1) % 64) < 32
    scale = HEAD_DIM ** -0.5
    for h in range(q_ref.shape[1] // HEAD_DIM):
        lanes = slice(h * HEAD_DIM, (h + 1) * HEAD_DIM)
        y = _rope(_rms(acc_ref[:, lanes], gain_ref[...]), cos, sin, first_half)
        q_ref[:, lanes] = (y * scale).astype(BF16)
    acc_ref[...] = _dot(u_ref[...], w_ref[...])


def _proj_kv_kernel(u_ref, w_ref, cos_ref, sin_ref, gain_ref, kt_ref, v_ref):
    cos, sin = cos_ref[...], sin_ref[...]
    first_half = (lax.broadcasted_iota(jnp.int32, cos.shape, 1) % 64) < 32
    for h0, acc in _head_pairs(u_ref, w_ref, 2 * A_KV_W):
        for h in range(2):
            part = acc[:, h * HEAD_DIM:(h + 1) * HEAD_DIM]
            if h0 + h < A_KV_HEADS:
                y = _rope(_rms(part, gain_ref[...]), cos, sin, first_half)
                kt_ref[0, (h0 + h) * HEAD_DIM:(h0 + h + 1) * HEAD_DIM, :] = y.T.astype(BF16)
            else:
                hv = h0 + h - A_KV_HEADS
                v_ref[:, hv * HEAD_DIM:(hv + 1) * HEAD_DIM] = part.astype(BF16)


def _rope_tables(t):
    rows = t // GRID_W
    row_ids = jnp.repeat(jnp.arange(rows), GRID_W).astype(F32)
    col_ids = jnp.tile(jnp.arange(GRID_W), rows).astype(F32)
    inv = ROPE_THETA ** (-jnp.arange(0, ROPE_AXIS_DIM, 2, dtype=F32) / ROPE_AXIS_DIM)
    ang_r = row_ids[:, None] * inv[None, :]
    ang_c = col_ids[:, None] * inv[None, :]
    cos = jnp.concatenate([jnp.cos(ang_r)] * 2 + [jnp.cos(ang_c)] * 2, axis=-1)
    sin = jnp.concatenate([-jnp.sin(ang_r), jnp.sin(ang_r), -jnp.sin(ang_c), jnp.sin(ang_c)], axis=-1)
    return cos, sin


def _proj_a(u, w, q_gain, k_gain, *, bsz, bm):
    n, d = u.shape
    t = n // bsz
    assert t % bm == 0 and A_Q_W % PROJ_BN == 0
    tiles_per_seq = t // bm
    cos, sin = _rope_tables(t)
    u_spec = pl.BlockSpec((bm, d), lambda i, j: (i, 0))
    tab = pl.BlockSpec((bm, HEAD_DIM), lambda i, j: (i % tiles_per_seq, 0))
    vec = pl.BlockSpec((1, HEAD_DIM), lambda i, j: (0, 0))
    nq = A_Q_W // PROJ_BN
    steps = (n // bm) * nq
    cur = lambda s: jnp.minimum(s, steps - 1)
    fin = lambda s: jnp.maximum(s - 1, 0)
    fin_tab = pl.BlockSpec((bm, HEAD_DIM), lambda s: ((fin(s) // nq) % tiles_per_seq, 0))
    qp = pl.pallas_call(
        _proj_q_kernel,
        grid=(steps + 1,),
        in_specs=[pl.BlockSpec((bm, d), lambda s: (cur(s) // nq, 0)),
                  pl.BlockSpec((d, PROJ_BN), lambda s: (0, cur(s) % nq)),
                  fin_tab, fin_tab, pl.BlockSpec((1, HEAD_DIM), lambda s: (0, 0))],
        out_specs=pl.BlockSpec((bm, PROJ_BN), lambda s: (fin(s) // nq, fin(s) % nq)),
        out_shape=jax.ShapeDtypeStruct((n, A_Q_W), BF16),
        scratch_shapes=[pltpu.VMEM((bm, PROJ_BN), F32)],
        compiler_params=_cparams(("arbitrary",), 56),
        name="proj_q",
    )(u, w, cos, sin, q_gain.reshape(1, HEAD_DIM))
    kt, v = pl.pallas_call(
        _proj_kv_kernel,
        grid=(n // bm, 1),
        in_specs=[u_spec, pl.BlockSpec((d, 2 * A_KV_W), lambda i, j: (0, A_Q_W // (2 * A_KV_W))), tab, tab, vec],
        out_specs=(pl.BlockSpec((1, A_KV_W, bm), lambda i, j: (i // tiles_per_seq, 0, i % tiles_per_seq)),
                   pl.BlockSpec((bm, A_KV_W), lambda i, j: (i, 0))),
        out_shape=(jax.ShapeDtypeStruct((bsz, A_KV_W, t), BF16), jax.ShapeDtypeStruct((n, A_KV_W), BF16)),
        compiler_params=_cparams(("parallel", "arbitrary"), 48),
        name="proj_kv",
    )(u, w, cos, sin, k_gain.reshape(1, HEAD_DIM))
    return qp, kt, v


def _proj_cols_kernel(u_ref, w_ref, o_ref, *, sigmoid):
    acc = _dot(u_ref[...], w_ref[...])
    if sigmoid:
        acc = 0.5 * jnp.tanh(0.5 * acc) + 0.5
    o_ref[...] = acc.astype(o_ref.dtype)


def _proj_cols(u, w, *, col0, width, bm, sigmoid, name):
    n, d = u.shape
    bn = _tile(width, PROJ_BN)
    assert col0 % bn == 0
    return pl.pallas_call(
        functools.partial(_proj_cols_kernel, sigmoid=sigmoid),
        grid=(n // bm, width // bn),
        in_specs=[pl.BlockSpec((bm, d), lambda i, j: (i, 0)),
                  pl.BlockSpec((d, bn), lambda i, j: (0, col0 // bn + j))],
        out_specs=pl.BlockSpec((bm, bn), lambda i, j: (i, j)),
        out_shape=jax.ShapeDtypeStruct((n, width), BF16),
        compiler_params=_cparams(("parallel", "arbitrary"), 48),
        name=name,
    )(u, w)


def _proj_cls_kernel(u_ref, w_ref, o_ref, acc_ref, *, dilation):
    acc = _dot(u_ref[...], w_ref[...])
    rows = o_ref.shape[1]
    for c in range(acc_ref.shape[0]):
        lanes = slice(c * HEAD_DIM, (c + 1) * HEAD_DIM)
        acc_ref[c] = acc[:, lanes]
        for r in range(dilation):
            o_ref[r, :, lanes] = acc_ref[c, pl.ds(r, rows, stride=dilation), :].astype(o_ref.dtype)


def _proj_cls(u, w, *, col0, width, bm, dilation, name):
    n, d = u.shape
    assert col0 % PROJ_BN == 0
    return pl.pallas_call(
        functools.partial(_proj_cls_kernel, dilation=dilation),
        grid=(n // bm, width // PROJ_BN),
        in_specs=[pl.BlockSpec((bm, d), lambda i, j: (i, 0)),
                  pl.BlockSpec((d, PROJ_BN), lambda i, j: (0, col0 // PROJ_BN + j))],
        out_specs=pl.BlockSpec((dilation, bm // dilation, PROJ_BN), lambda i, j: (0, i, j)),
        out_shape=jax.ShapeDtypeStruct((dilation, n // dilation, width), BF16),
        scratch_shapes=[pltpu.VMEM((PROJ_BN // HEAD_DIM, bm, HEAD_DIM), F32)],
        compiler_params=_cparams(("parallel", "arbitrary"), 56),
        name=name,
    )(u, w)


def _attn_a_kernel(q_ref, kt_ref, v_ref, o_ref):
    kt = kt_ref[0]
    v = v_ref[0]
    for r0 in range(0, q_ref.shape[1], ATTN_A_ROWS):
        rows = slice(r0, r0 + ATTN_A_ROWS)
        for g in range(A_GROUP):
            sl = slice(g * HEAD_DIM, (g + 1) * HEAD_DIM)
            s = _dot(q_ref[0, rows, sl], kt)
            p = jnp.exp(s - jnp.max(s, axis=-1, keepdims=True))
            denom = jnp.sum(p, axis=-1, keepdims=True)
            o = _dot(p.astype(BF16), v) / denom
            o_ref[0, rows, sl] = o.astype(o_ref.dtype)


def _attn_a(qp, kt, v, *, tq):
    b, t, _ = qp.shape
    gw = A_GROUP * HEAD_DIM
    assert tq % ATTN_A_ROWS == 0
    return pl.pallas_call(
        _attn_a_kernel,
        grid=(b, A_KV_HEADS, t // tq),
        in_specs=[pl.BlockSpec((1, tq, gw), lambda bi, kv, i: (bi, i, kv)),
                  pl.BlockSpec((1, HEAD_DIM, t), lambda bi, kv, i: (bi, kv, 0)),
                  pl.BlockSpec((1, t, HEAD_DIM), lambda bi, kv, i: (bi, 0, kv))],
        out_specs=pl.BlockSpec((1, tq, gw), lambda bi, kv, i: (bi, i, kv)),
        out_shape=jax.ShapeDtypeStruct((b, t, A_Q_W), BF16),
        compiler_params=_cparams(("parallel", "parallel", "arbitrary"), 48),
        name="attn_a",
    )(qp, kt, v)


def _attn_b_kernel(slopes_ref, q_ref, k_ref, v_ref, o_ref, lse_ref, *, group, dilation, half, tq, kw,
                   heads, interleave):
    length = q_ref.shape[0]
    head0 = group * B_HEADS_PER_GROUP + pl.program_id(1) * heads
    scale = HEAD_DIM ** -0.5
    row = lax.broadcasted_iota(jnp.int32, (tq, kw), 0)
    col = lax.broadcasted_iota(jnp.int32, (tq, kw), 1)

    def tile(i, lanes, slope):
        m0 = pl.multiple_of(i * tq, tq)
        ks = pl.multiple_of(jnp.clip(m0 - half, 0, length - kw), half)
        q = q_ref[pl.ds(m0, tq), lanes]
        k = k_ref[pl.ds(ks, kw), lanes]
        v = v_ref[pl.ds(ks, kw), lanes]
        s = lax.dot_general(q, k, (((1,), (1,)), ((), ())), preferred_element_type=F32) * scale
        dist = jnp.abs(col - row + (ks - m0))
        bias = -slope * (dist * dilation).astype(F32)
        s = jnp.where(dist <= half, s + bias, NEG_INF)
        m = jnp.max(s, axis=-1, keepdims=True)
        p = jnp.exp(s - m)
        denom = jnp.sum(p, axis=-1, keepdims=True)
        o = _dot(p.astype(BF16), v) / denom
        o_ref[pl.ds(m0, tq), lanes] = o.astype(o_ref.dtype)
        lse_ref[pl.ds(m0, tq), lanes] = jnp.broadcast_to(m + jnp.log(denom), (tq, HEAD_DIM))

    for hh in range(heads):
        lanes = slice(hh * HEAD_DIM, (hh + 1) * HEAD_DIM)
        slope = slopes_ref[head0 + hh]

        def body(it, carry, lanes=lanes, slope=slope):
            for u in range(interleave):
                tile(it * interleave + u, lanes, slope)
            return carry

        lax.fori_loop(0, length // (tq * interleave), body, 0)


def _attn_b(qkv, slopes, group, *, bsz):
    window, dilation = B_PATTERNS[group]
    length = qkv.shape[-2] // (1 if dilation == 1 else bsz)
    half = (window // 2) // dilation
    tq = min(128, length)
    kw = min(length, tq + 2 * half)
    heads = 1 if dilation == 1 else B_HEADS_PER_GROUP
    steps = B_HEADS_PER_GROUP // heads
    width = heads * HEAD_DIM

    if dilation == 1:
        in_spec = lambda part: pl.BlockSpec((None, length, width), lambda bi, h, r, s: (bi, 0, part * steps + h))
        out_spec = pl.BlockSpec((None, length, width), lambda bi, h, r, s: (bi, 0, h))
        out_dims = (bsz, length, B_OUT_W)
    else:
        in_spec = lambda part: pl.BlockSpec((None, length, width), lambda bi, h, r, s: (r, bi, part * steps + h))
        out_spec = pl.BlockSpec((None, length, width), lambda bi, h, r, s: (r, bi, h))
        out_dims = (dilation, bsz * length, B_OUT_W)
    n_tiles = length // tq
    return pl.pallas_call(
        functools.partial(_attn_b_kernel, group=group, dilation=dilation, half=half, tq=tq, kw=kw,
                          heads=heads, interleave=min(8, n_tiles)),
        grid_spec=pltpu.PrefetchScalarGridSpec(
            num_scalar_prefetch=1,
            grid=(bsz, steps, dilation),
            in_specs=[in_spec(0), in_spec(1), in_spec(2)],
            out_specs=(out_spec, out_spec)),
        out_shape=(jax.ShapeDtypeStruct(out_dims, BF16), jax.ShapeDtypeStruct(out_dims, F32)),
        compiler_params=_cparams(("parallel", "parallel", "parallel"), 40),
        name=f"attn_b{group}",
    )(slopes, qkv, qkv, qkv)


def _merge_b_kernel(o0_ref, l0_ref, o1_ref, l1_ref, o2_ref, l2_ref, yb_ref, on1_ref, ln1_ref, on2_ref, ln2_ref):
    for o_ref, l_ref, on_ref, ln_ref in ((o1_ref, l1_ref, on1_ref, ln1_ref),
                                         (o2_ref, l2_ref, on2_ref, ln2_ref)):
        dilation, rows = o_ref.shape[0], o_ref.shape[1]
        for c in range(on_ref.shape[0]):
            lanes = slice(c * HEAD_DIM, (c + 1) * HEAD_DIM)
            for r in range(dilation):
                on_ref[c, pl.ds(r, rows, stride=dilation), :] = o_ref[r, :, lanes].astype(F32)
                ln_ref[c, pl.ds(r, rows, stride=dilation), :] = l_ref[r, :, lanes]
    for c in range(on1_ref.shape[0]):
        lanes = slice(c * HEAD_DIM, (c + 1) * HEAD_DIM)
        l0, l1, l2 = l0_ref[:, lanes], ln1_ref[c], ln2_ref[c]
        m = jnp.maximum(jnp.maximum(l0, l1), l2)
        e0, e1, e2 = jnp.exp(l0 - m), jnp.exp(l1 - m), jnp.exp(l2 - m)
        tot = e0 + e1 + e2
        yb = (e0 / tot) * o0_ref[:, lanes].astype(F32) + (e1 / tot) * on1_ref[c] + (e2 / tot) * on2_ref[c]
        yb_ref[:, lanes] = yb.astype(BF16)


def _merge_b(o_l, *, bm):
    (o0, l0), (o1, l1), (o2, l2) = o_l
    n = o0.shape[0]
    row = pl.BlockSpec((bm, B_OUT_W), lambda i: (i, 0))

    def cls(arr):
        dil = arr.shape[0]
        return pl.BlockSpec((dil, bm // dil, B_OUT_W), lambda i: (0, i, 0))

    return pl.pallas_call(
        _merge_b_kernel,
        grid=(n // bm,),
        in_specs=[row, row, cls(o1), cls(l1), cls(o2), cls(l2)],
        out_specs=row,
        out_shape=jax.ShapeDtypeStruct((n, B_OUT_W), BF16),
        scratch_shapes=[pltpu.VMEM((B_HEADS_PER_GROUP, bm, HEAD_DIM), F32)] * 4,
        compiler_params=_cparams(("parallel",), 40),
        name="merge_b",
    )(o0, l0, o1, l1, o2, l2)


def _branch_kernel(ya_ref, yb_ref, wa_ref, wb_ref, ga_ref, gb_ref, out_ref):
    a = _dot(ya_ref[...], wa_ref[...])
    bb = _dot(yb_ref[...], wb_ref[...])
    merged = ga_ref[...].astype(F32) * a + gb_ref[...].astype(F32) * bb
    out_ref[...] = merged.astype(out_ref.dtype)


def _branch(ya, yb, wa, wb, gates, *, bm, bn):
    n, d = ya.shape[0], wa.shape[1]
    row = lambda w: pl.BlockSpec((bm, w), lambda i, j: (i, 0))
    return pl.pallas_call(
        _branch_kernel,
        grid=(n // bm, d // bn),
        in_specs=[row(A_Q_W), row(B_OUT_W),
                  pl.BlockSpec((A_Q_W, bn), lambda i, j: (0, j)),
                  pl.BlockSpec((B_OUT_W, bn), lambda i, j: (0, j)),
                  pl.BlockSpec((bm, bn), lambda i, j: (i, j)),
                  pl.BlockSpec((bm, bn), lambda i, j: (i, d // bn + j))],
        out_specs=pl.BlockSpec((bm, bn), lambda i, j: (i, j)),
        out_shape=jax.ShapeDtypeStruct((n, d), BF16),
        compiler_params=_cparams(("parallel", "arbitrary"), 48),
        name="branch_proj",
    )(ya, yb, wa, wb, gates, gates)


def _out_kernel(h_ref, a_ref, w_ref, o_ref):
    o_ref[...] = h_ref[...] + _dot(a_ref[...], w_ref[...])


def _out_proj(h, a, w, *, bm, bn):
    n, d = h.shape
    k = a.shape[1]
    return pl.pallas_call(
        _out_kernel,
        grid=(n // bm, d // bn),
        in_specs=[pl.BlockSpec((bm, bn), lambda i, j: (i, j)),
                  pl.BlockSpec((bm, k), lambda i, j: (i, 0)),
                  pl.BlockSpec((k, bn), lambda i, j: (0, j))],
        out_specs=pl.BlockSpec((bm, bn), lambda i, j: (i, j)),
        out_shape=jax.ShapeDtypeStruct((n, d), F32),
        compiler_params=_cparams(("parallel", "arbitrary"), 56),
        name="out_proj",
    )(h, a, w)


def _layer(h, g_ffn1, w1_gate, w1_up, w1_down, g_mix, w_in, q_norm_a, k_norm_a,
           w_branch_a, w_branch_b, w_out, g_ffn2, w2_gate, w2_up, w2_down, g_next, *, bsz, last):
    n, d = h.shape
    t = n // bsz
    c = lambda w: w.astype(BF16)
    bm = _tile(n, 512)
    ffn = functools.partial(_ffn, bm=_tile(n, 1024), tf=_tile(w1_gate.shape[1], 256), io_rows=_tile(n, 128))

    later = [w2_gate, w2_up, w2_down, w_in, w_branch_a, w_branch_b, w_out]
    row_tiles, col_steps = n // ffn.keywords["bm"], w1_gate.shape[1] // ffn.keywords["tf"]
    rides = [_cast_block(w.shape, row_tiles, col_steps) is not None for w in later]
    h, u, *cast = ffn(h, g_ffn1, c(w1_gate), c(w1_up), c(w1_down), g_mix, emit_hidden=True,
                      cast_along=[w for w, ok in zip(later, rides) if ok])
    cast = iter(cast)
    *ffn2_w, w_in, w_branch_a, w_branch_b, w_out = [next(cast) if ok else c(w) for w, ok in zip(later, rides)]

    pm = _tile(t, 1024)
    a_w = A_Q_W + 2 * A_KV_W
    qp, kt, va = _proj_a(u, w_in, q_norm_a, k_norm_a, bsz=bsz, bm=pm)
    qkv_b = [_proj_cols(u, w_in, col0=a_w, width=B_GROUP_W, bm=pm, sigmoid=False,
                        name="proj_b0").reshape(bsz, t, B_GROUP_W)]
    for g in range(1, B_N_GROUPS):
        qkv_b.append(_proj_cls(u, w_in, col0=a_w + g * B_GROUP_W, width=B_GROUP_W, bm=pm,
                               dilation=B_PATTERNS[g][1], name=f"proj_b{g}"))
    gates = _proj_cols(u, w_in, col0=a_w + B_QKV_W, width=2 * d, bm=pm, sigmoid=True, name="proj_gates")

    ya = _attn_a(qp.reshape(bsz, t, A_Q_W), kt, va.reshape(bsz, t, A_KV_W), tq=_tile(t, 1024)).reshape(n, A_Q_W)

    slopes = jnp.exp2(-8.0 * jnp.arange(1, B_HEADS + 1, dtype=F32) / B_HEADS)
    o_l = [_attn_b(qkv_b[g], slopes, g, bsz=bsz) for g in range(B_N_GROUPS)]
    o_l[0] = tuple(a.reshape(n, B_OUT_W) for a in o_l[0])

    yb = _merge_b(o_l, bm=bm)
    merged = _branch(ya, yb, w_branch_a, w_branch_b, gates, bm=_tile(n, 1024), bn=_tile(d, 1024))
    h = _out_proj(h, merged, w_out, bm=_tile(n, 1024), bn=_tile(d, 1024))

    if last:
        return ffn(h, g_ffn2, *ffn2_w, g_next, emit_hidden=False, io_rows=_tile(n, 256))
    return ffn(h, g_ffn2, *ffn2_w, g_next, emit_hidden=True)[0]


def kernel(x, g_ffn1, w1_gate, w1_up, w1_down, g_mix, w_in, q_norm_a, k_norm_a, w_branch_a, w_branch_b,
           w_out, g_ffn2, w2_gate, w2_up, w2_down, g_final):
    bsz, t, d = x.shape
    depth = g_ffn1.shape[0]
    h = x.reshape(bsz * t, d)
    for l in range(depth):
        last = l == depth - 1
        h = _layer(h, g_ffn1[l], w1_gate[l], w1_up[l], w1_down[l], g_mix[l], w_in[l], q_norm_a[l],
                   k_norm_a[l], w_branch_a[l], w_branch_b[l], w_out[l], g_ffn2[l], w2_gate[l], w2_up[l],
                   w2_down[l], g_final if last else g_ffn1[l + 1], bsz=bsz, last=last)
    return h.reshape(bsz, t, d)
```

```python
import functools

import jax
import jax.numpy as jnp
from jax import lax
from jax.experimental import pallas as pl
from jax.experimental.pallas import tpu as pltpu

HEAD_DIM = 128
A_HEADS = 16
A_KV_HEADS = 4
A_GROUP = A_HEADS // A_KV_HEADS
B_PATTERNS = ((128, 1), (512, 4), (2048, 16))
B_HEADS_PER_GROUP = 8
B_N_GROUPS = len(B_PATTERNS)
B_HEADS = B_N_GROUPS * B_HEADS_PER_GROUP
A_Q_W = A_HEADS * HEAD_DIM
A_KV_W = A_KV_HEADS * HEAD_DIM
B_OUT_W = B_HEADS_PER_GROUP * HEAD_DIM
B_GROUP_W = 3 * B_OUT_W
B_QKV_W = B_N_GROUPS * B_GROUP_W
GRID_W = 64
ROPE_THETA = 10000.0
ROPE_AXIS_DIM = HEAD_DIM // 2
RMS_EPS = 1e-6
NEG_INF = -1e30

DOWN_CHUNK = 512
FFN_MM_ROWS = 512
ATTN_A_ROWS = 256
PROJ_BN = 2 * A_KV_W
LANES = 128
BF16_SUBLANES = 16
MIB = 1024 * 1024
V7X_VMEM_MIB = 64
VMEM_FFN_MIB = V7X_VMEM_MIB - 4
VMEM_WIDE_MIB = V7X_VMEM_MIB - 8
VMEM_MM_MIB = V7X_VMEM_MIB - 16
VMEM_SMALL_MIB = V7X_VMEM_MIB - 24
BF16 = jnp.bfloat16
F32 = jnp.float32


def _cparams(sem, vmem_mib):
    return pltpu.CompilerParams(dimension_semantics=sem, vmem_limit_bytes=vmem_mib * MIB)


def _tile(n, pref):
    t = min(n, pref)
    while n % t:
        t //= 2
    return t


def _rms(x, gain):
    y = x * lax.rsqrt(jnp.mean(x * x, axis=-1, keepdims=True) + RMS_EPS)
    return y * gain


def _dot(a, b):
    return jnp.dot(a, b, preferred_element_type=F32)


def _ffn_kernel(x_ref, gin_ref, wg_ref, wu_ref, wd_ref, gout_ref, *rest, emit_hidden, n_cast, n_io, n_f):
    cast_in, rest = rest[:n_cast], rest[n_cast:]
    outs, rest = rest[:2 if emit_hidden else 1], rest[2 if emit_hidden else 1:]
    cast_out, (xn_ref, acc_ref) = rest[:n_cast], rest[n_cast:]
    y_ref = outs[-1]
    io_rows = x_ref.shape[0]
    j = pl.program_id(1)
    first_mm, last_mm = n_io - 1, n_io + n_f - 2

    for src_ref, dst_ref in zip(cast_in, cast_out):
        dst_ref[...] = src_ref[...].astype(dst_ref.dtype)

    @pl.when(j <= first_mm)
    def _():
        rows = pl.ds(pl.multiple_of(j * io_rows, io_rows), io_rows)
        x = x_ref[...]
        xn_ref[rows, :] = _rms(x, gin_ref[...]).astype(BF16)
        acc_ref[rows, :] = x

    @pl.when((j >= first_mm) & (j <= last_mm))
    def _():
        for r0 in range(0, acc_ref.shape[0], FFN_MM_ROWS):
            rows = slice(r0, r0 + FFN_MM_ROWS)
            xn = xn_ref[rows, :]
            gate = _dot(xn, wg_ref[...])
            up = _dot(xn, wu_ref[...])
            act = ((0.25 * gate) * (1.0 + jnp.tanh(0.5 * gate)) * up).astype(BF16)
            for c in range(0, acc_ref.shape[1], DOWN_CHUNK):
                acc_ref[rows, c:c + DOWN_CHUNK] += _dot(act, wd_ref[:, c:c + DOWN_CHUNK])

    @pl.when(j >= last_mm)
    def _():
        rows = pl.ds(pl.multiple_of((j - last_mm) * io_rows, io_rows), io_rows)
        h = acc_ref[rows, :]
        if emit_hidden:
            outs[0][...] = h
        y_ref[...] = _rms(h, gout_ref[...]).astype(y_ref.dtype)


def _cast_block(shape, row_tiles, col_steps):
    rows, cols = shape
    if rows % row_tiles or (rows // row_tiles) % BF16_SUBLANES:
        return None
    for width in range(LANES, cols + 1, LANES):
        if cols % width == 0 and cols // width <= col_steps:
            return rows // row_tiles, width
    return None


def _ffn(x, g_in, wg, wu, wd, g_out, *, emit_hidden, bm, tf, io_rows, cast_along=()):
    n, d = x.shape
    f = wg.shape[1]
    n_io, n_f = bm // io_rows, f // tf
    assert bm % FFN_MM_ROWS == 0 and bm % io_rows == 0
    grid = (n // bm, n_f + 2 * (n_io - 1))
    fblk = lambda j: jnp.clip(j - (n_io - 1), 0, n_f - 1)
    vec = pl.BlockSpec((1, d), lambda i, j: (0, 0))
    in_specs = [pl.BlockSpec((io_rows, d), lambda i, j: (i * n_io + jnp.minimum(j, n_io - 1), 0)), vec,
                pl.BlockSpec((d, tf), lambda i, j: (0, fblk(j))),
                pl.BlockSpec((d, tf), lambda i, j: (0, fblk(j))),
                pl.BlockSpec((tf, d), lambda i, j: (fblk(j), 0)),
                vec]
    out_row = pl.BlockSpec((io_rows, d), lambda i, j: (i * n_io + jnp.clip(j - (n_io + n_f - 2), 0, n_io - 1), 0))
    if emit_hidden:
        out_shape = [jax.ShapeDtypeStruct((n, d), F32), jax.ShapeDtypeStruct((n, d), BF16)]
        out_specs = [out_row, out_row]
    else:
        out_shape = [jax.ShapeDtypeStruct((n, d), F32)]
        out_specs = [out_row]
    for w in cast_along:
        blk = _cast_block(w.shape, grid[0], n_f)
        last = w.shape[1] // blk[1] - 1
        spec = pl.BlockSpec(blk, lambda i, j, last=last: (i, jnp.minimum(fblk(j), last)))
        in_specs.append(spec)
        out_specs.append(spec)
        out_shape.append(jax.ShapeDtypeStruct(w.shape, BF16))
    outs = pl.pallas_call(
        functools.partial(_ffn_kernel, emit_hidden=emit_hidden, n_cast=len(cast_along), n_io=n_io, n_f=n_f),
        grid=grid, in_specs=in_specs, out_specs=out_specs, out_shape=out_shape,
        scratch_shapes=[pltpu.VMEM((bm, d), BF16), pltpu.VMEM((bm, d), F32)],
        compiler_params=_cparams(("parallel", "arbitrary"), VMEM_FFN_MIB),
        name="ffn_hidden" if emit_hidden else "ffn_final",
    )(x, g_in.reshape(1, d), wg, wu, wd, g_out.reshape(1, d), *cast_along)
    return outs if len(outs) > 1 else outs[0]


def _rope(y, cos, sin_signed, first_half):
    partner = jnp.where(first_half, pltpu.roll(y, HEAD_DIM - 32, axis=1), pltpu.roll(y, 32, axis=1))
    return y * cos + partner * sin_signed


def _head_pairs(u_ref, w_ref, width):
    u = u_ref[...]
    for c in range(0, width, 2 * HEAD_DIM):
        yield c // HEAD_DIM, _dot(u, w_ref[:, c:c + 2 * HEAD_DIM])


def _proj_q_kernel(u_ref, w_ref, cos_ref, sin_ref, gain_ref, q_ref, acc_ref):
    @pl.when(pl.program_id(0) == 0)
    def _():
        acc_ref[...] = jnp.zeros(acc_ref.shape, F32)

    cos, sin = cos_ref[...], sin_ref[...]
    first_half = (lax.broadcasted_iota(jnp.int32, cos.shape, 1) % 64) < 32
    scale = HEAD_DIM ** -0.5
    for h in range(q_ref.shape[1] // HEAD_DIM):
        lanes = slice(h * HEAD_DIM, (h + 1) * HEAD_DIM)
        y = _rope(_rms(acc_ref[:, lanes], gain_ref[...]), cos, sin, first_half)
        q_ref[:, lanes] = (y * scale).astype(BF16)
    acc_ref[...] = _dot(u_ref[...], w_ref[...])


def _proj_kv_kernel(u_ref, w_ref, cos_ref, sin_ref, gain_ref, kt_ref, v_ref):
    cos, sin = cos_ref[...], sin_ref[...]
    first_half = (lax.broadcasted_iota(jnp.int32, cos.shape, 1) % 64) < 32
    for h0, acc in _head_pairs(u_ref, w_ref, 2 * A_KV_W):
        for h in range(2):
            part = acc[:, h * HEAD_DIM:(h + 1) * HEAD_DIM]
            if h0 + h < A_KV_HEADS:
                y = _rope(_rms(part, gain_ref[...]), cos, sin, first_half)
                kt_ref[0, (h0 + h) * HEAD_DIM:(h0 + h + 1) * HEAD_DIM, :] = y.T.astype(BF16)
            else:
                hv = h0 + h - A_KV_HEADS
                v_ref[:, hv * HEAD_DIM:(hv + 1) * HEAD_DIM] = part.astype(BF16)


def _rope_tables(t):
    rows = t // GRID_W
    row_ids = jnp.repeat(jnp.arange(rows), GRID_W).astype(F32)
    col_ids = jnp.tile(jnp.arange(GRID_W), rows).astype(F32)
    inv = ROPE_THETA ** (-jnp.arange(0, ROPE_AXIS_DIM, 2, dtype=F32) / ROPE_AXIS_DIM)
    ang_r = row_ids[:, None] * inv[None, :]
    ang_c = col_ids[:, None] * inv[None, :]
    cos = jnp.concatenate([jnp.cos(ang_r)] * 2 + [jnp.cos(ang_c)] * 2, axis=-1)
    sin = jnp.concatenate([-jnp.sin(ang_r), jnp.sin(ang_r), -jnp.sin(ang_c), jnp.sin(ang_c)], axis=-1)
    return cos, sin


def _proj_a(u, w, q_gain, k_gain, *, bsz, bm):
    n, d = u.shape
    t = n // bsz
    assert t % bm == 0 and A_Q_W % PROJ_BN == 0
    tiles_per_seq = t // bm
    cos, sin = _rope_tables(t)
    u_spec = pl.BlockSpec((bm, d), lambda i, j: (i, 0))
    tab = pl.BlockSpec((bm, HEAD_DIM), lambda i, j: (i % tiles_per_seq, 0))
    vec = pl.BlockSpec((1, HEAD_DIM), lambda i, j: (0, 0))
    nq = A_Q_W // PROJ_BN
    steps = (n // bm) * nq
    cur = lambda s: jnp.minimum(s, steps - 1)
    fin = lambda s: jnp.maximum(s - 1, 0)
    fin_tab = pl.BlockSpec((bm, HEAD_DIM), lambda s: ((fin(s) // nq) % tiles_per_seq, 0))
    qp = pl.pallas_call(
        _proj_q_kernel,
        grid=(steps + 1,),
        in_specs=[pl.BlockSpec((bm, d), lambda s: (cur(s) // nq, 0)),
                  pl.BlockSpec((d, PROJ_BN), lambda s: (0, cur(s) % nq)),
                  fin_tab, fin_tab, pl.BlockSpec((1, HEAD_DIM), lambda s: (0, 0))],
        out_specs=pl.BlockSpec((bm, PROJ_BN), lambda s: (fin(s) // nq, fin(s) % nq)),
        out_shape=jax.ShapeDtypeStruct((n, A_Q_W), BF16),
        scratch_shapes=[pltpu.VMEM((bm, PROJ_BN), F32)],
        compiler_params=_cparams(("arbitrary",), VMEM_WIDE_MIB),
        name="proj_q",
    )(u, w, cos, sin, q_gain.reshape(1, HEAD_DIM))
    kt, v = pl.pallas_call(
        _proj_kv_kernel,
        grid=(n // bm, 1),
        in_specs=[u_spec, pl.BlockSpec((d, 2 * A_KV_W), lambda i, j: (0, A_Q_W // (2 * A_KV_W))), tab, tab, vec],
        out_specs=(pl.BlockSpec((1, A_KV_W, bm), lambda i, j: (i // tiles_per_seq, 0, i % tiles_per_seq)),
                   pl.BlockSpec((bm, A_KV_W), lambda i, j: (i, 0))),
        out_shape=(jax.ShapeDtypeStruct((bsz, A_KV_W, t), BF16), jax.ShapeDtypeStruct((n, A_KV_W), BF16)),
        compiler_params=_cparams(("parallel", "arbitrary"), VMEM_MM_MIB),
        name="proj_kv",
    )(u, w, cos, sin, k_gain.reshape(1, HEAD_DIM))
    return qp, kt, v


def _proj_cols_kernel(u_ref, w_ref, o_ref, *, sigmoid):
    acc = _dot(u_ref[...], w_ref[...])
    if sigmoid:
        acc = 0.5 * jnp.tanh(0.5 * acc) + 0.5
    o_ref[...] = acc.astype(o_ref.dtype)


def _proj_cols(u, w, *, col0, width, bm, sigmoid, name):
    n, d = u.shape
    bn = _tile(width, PROJ_BN)
    assert col0 % bn == 0
    return pl.pallas_call(
        functools.partial(_proj_cols_kernel, sigmoid=sigmoid),
        grid=(n // bm, width // bn),
        in_specs=[pl.BlockSpec((bm, d), lambda i, j: (i, 0)),
                  pl.BlockSpec((d, bn), lambda i, j: (0, col0 // bn + j))],
        out_specs=pl.BlockSpec((bm, bn), lambda i, j: (i, j)),
        out_shape=jax.ShapeDtypeStruct((n, width), BF16),
        compiler_params=_cparams(("parallel", "arbitrary"), VMEM_MM_MIB),
        name=name,
    )(u, w)


def _proj_cls_kernel(u_ref, w_ref, o_ref, acc_ref, *, dilation):
    acc = _dot(u_ref[...], w_ref[...])
    rows = o_ref.shape[1]
    for c in range(acc_ref.shape[0]):
        lanes = slice(c * HEAD_DIM, (c + 1) * HEAD_DIM)
        acc_ref[c] = acc[:, lanes]
        for r in range(dilation):
            o_ref[r, :, lanes] = acc_ref[c, pl.ds(r, rows, stride=dilation), :].astype(o_ref.dtype)


def _proj_cls(u, w, *, col0, width, bm, dilation, name):
    n, d = u.shape
    assert col0 % PROJ_BN == 0
    return pl.pallas_call(
        functools.partial(_proj_cls_kernel, dilation=dilation),
        grid=(n // bm, width // PROJ_BN),
        in_specs=[pl.BlockSpec((bm, d), lambda i, j: (i, 0)),
                  pl.BlockSpec((d, PROJ_BN), lambda i, j: (0, col0 // PROJ_BN + j))],
        out_specs=pl.BlockSpec((dilation, bm // dilation, PROJ_BN), lambda i, j: (0, i, j)),
        out_shape=jax.ShapeDtypeStruct((dilation, n // dilation, width), BF16),
        scratch_shapes=[pltpu.VMEM((PROJ_BN // HEAD_DIM, bm, HEAD_DIM), F32)],
        compiler_params=_cparams(("parallel", "arbitrary"), VMEM_WIDE_MIB),
        name=name,
    )(u, w)


def _attn_a_kernel(q_ref, kt_ref, v_ref, o_ref, v1_ref):
    @pl.when(pl.program_id(2) == 0)
    def _():
        v1_ref[:, :HEAD_DIM] = v_ref[0]
        v1_ref[:, HEAD_DIM:] = jnp.ones((v1_ref.shape[0], HEAD_DIM), BF16)

    kt = kt_ref[0]
    v1 = v1_ref[...]
    for r0 in range(0, q_ref.shape[1], ATTN_A_ROWS):
        rows = slice(r0, r0 + ATTN_A_ROWS)
        for g in range(A_GROUP):
            sl = slice(g * HEAD_DIM, (g + 1) * HEAD_DIM)
            s = _dot(q_ref[0, rows, sl], kt)
            p = jnp.exp(s - jnp.max(s, axis=-1, keepdims=True)).astype(BF16)
            ov = _dot(p, v1)
            o_ref[0, rows, sl] = (ov[:, :HEAD_DIM] / ov[:, HEAD_DIM:]).astype(o_ref.dtype)


def _attn_a(qp, kt, v, *, tq):
    b, t, _ = qp.shape
    gw = A_GROUP * HEAD_DIM
    assert tq % ATTN_A_ROWS == 0
    return pl.pallas_call(
        _attn_a_kernel,
        grid=(b, A_KV_HEADS, t // tq),
        in_specs=[pl.BlockSpec((1, tq, gw), lambda bi, kv, i: (bi, i, kv)),
                  pl.BlockSpec((1, HEAD_DIM, t), lambda bi, kv, i: (bi, kv, 0)),
                  pl.BlockSpec((1, t, HEAD_DIM), lambda bi, kv, i: (bi, 0, kv))],
        out_specs=pl.BlockSpec((1, tq, gw), lambda bi, kv, i: (bi, i, kv)),
        out_shape=jax.ShapeDtypeStruct((b, t, A_Q_W), BF16),
        scratch_shapes=[pltpu.VMEM((t, 2 * HEAD_DIM), BF16)],
        compiler_params=_cparams(("parallel", "parallel", "arbitrary"), VMEM_MM_MIB),
        name="attn_a",
    )(qp, kt, v)


def _attn_b_kernel(slopes_ref, q_ref, k_ref, v_ref, o_ref, lse_ref, *, group, dilation, half, tq, kw,
                   heads, interleave):
    length = q_ref.shape[0]
    head0 = group * B_HEADS_PER_GROUP + pl.program_id(1) * heads
    scale = HEAD_DIM ** -0.5
    row = lax.broadcasted_iota(jnp.int32, (tq, kw), 0)
    col = lax.broadcasted_iota(jnp.int32, (tq, kw), 1)

    def tile(i, lanes, slope):
        m0 = pl.multiple_of(i * tq, tq)
        ks = pl.multiple_of(jnp.clip(m0 - half, 0, length - kw), half)
        q = q_ref[pl.ds(m0, tq), lanes]
        k = k_ref[pl.ds(ks, kw), lanes]
        v = v_ref[pl.ds(ks, kw), lanes]
        s = lax.dot_general(q, k, (((1,), (1,)), ((), ())), preferred_element_type=F32) * scale
        dist = jnp.abs(col - row + (ks - m0))
        bias = -slope * (dist * dilation).astype(F32)
        s = jnp.where(dist <= half, s + bias, NEG_INF)
        m = jnp.max(s, axis=-1, keepdims=True)
        p = jnp.exp(s - m)
        denom = jnp.sum(p, axis=-1, keepdims=True)
        o = _dot(p.astype(BF16), v) / denom
        o_ref[pl.ds(m0, tq), lanes] = o.astype(o_ref.dtype)
        lse_ref[pl.ds(m0, tq), lanes] = jnp.broadcast_to(m + jnp.log(denom), (tq, HEAD_DIM))

    for hh in range(heads):
        lanes = slice(hh * HEAD_DIM, (hh + 1) * HEAD_DIM)
        slope = slopes_ref[head0 + hh]

        def body(it, carry, lanes=lanes, slope=slope):
            for u in range(interleave):
                tile(it * interleave + u, lanes, slope)
            return carry

        lax.fori_loop(0, length // (tq * interleave), body, 0)


def _attn_b(qkv, slopes, group, *, bsz):
    window, dilation = B_PATTERNS[group]
    length = qkv.shape[-2] // (1 if dilation == 1 else bsz)
    half = (window // 2) // dilation
    tq = min(128, length)
    kw = min(length, tq + 2 * half)
    heads = 1 if dilation == 1 else B_HEADS_PER_GROUP
    steps = B_HEADS_PER_GROUP // heads
    width = heads * HEAD_DIM

    if dilation == 1:
        in_spec = lambda part: pl.BlockSpec((None, length, width), lambda bi, h, r, s: (bi, 0, part * steps + h))
        out_spec = pl.BlockSpec((None, length, width), lambda bi, h, r, s: (bi, 0, h))
        out_dims = (bsz, length, B_OUT_W)
    else:
        in_spec = lambda part: pl.BlockSpec((None, length, width), lambda bi, h, r, s: (r, bi, part * steps + h))
        out_spec = pl.BlockSpec((None, length, width), lambda bi, h, r, s: (r, bi, h))
        out_dims = (dilation, bsz * length, B_OUT_W)
    n_tiles = length // tq
    return pl.pallas_call(
        functools.partial(_attn_b_kernel, group=group, dilation=dilation, half=half, tq=tq, kw=kw,
                          heads=heads, interleave=min(8, n_tiles)),
        grid_spec=pltpu.PrefetchScalarGridSpec(
            num_scalar_prefetch=1,
            grid=(bsz, steps, dilation),
            in_specs=[in_spec(0), in_spec(1), in_spec(2)],
            out_specs=(out_spec, out_spec)),
        out_shape=(jax.ShapeDtypeStruct(out_dims, BF16), jax.ShapeDtypeStruct(out_dims, F32)),
        compiler_params=_cparams(("parallel", "parallel", "parallel"), VMEM_SMALL_MIB),
        name=f"attn_b{group}",
    )(slopes, qkv, qkv, qkv)


def _merge_b_kernel(o0_ref, l0_ref, o1_ref, l1_ref, o2_ref, l2_ref, yb_ref, on1_ref, ln1_ref, on2_ref, ln2_ref):
    for o_ref, l_ref, on_ref, ln_ref in ((o1_ref, l1_ref, on1_ref, ln1_ref),
                                         (o2_ref, l2_ref, on2_ref, ln2_ref)):
        dilation, rows = o_ref.shape[0], o_ref.shape[1]
        for c in range(on_ref.shape[0]):
            lanes = slice(c * HEAD_DIM, (c + 1) * HEAD_DIM)
            for r in range(dilation):
                on_ref[c, pl.ds(r, rows, stride=dilation), :] = o_ref[r, :, lanes].astype(F32)
                ln_ref[c, pl.ds(r, rows, stride=dilation), :] = l_ref[r, :, lanes]
    for c in range(on1_ref.shape[0]):
        lanes = slice(c * HEAD_DIM, (c + 1) * HEAD_DIM)
        l0, l1, l2 = l0_ref[:, lanes], ln1_ref[c], ln2_ref[c]
        m = jnp.maximum(jnp.maximum(l0, l1), l2)
        e0, e1, e2 = jnp.exp(l0 - m), jnp.exp(l1 - m), jnp.exp(l2 - m)
        tot = e0 + e1 + e2
        yb = (e0 / tot) * o0_ref[:, lanes].astype(F32) + (e1 / tot) * on1_ref[c] + (e2 / tot) * on2_ref[c]
        yb_ref[:, lanes] = yb.astype(BF16)


def _merge_b(o_l, *, bm):
    (o0, l0), (o1, l1), (o2, l2) = o_l
    n = o0.shape[0]
    row = pl.BlockSpec((bm, B_OUT_W), lambda i: (i, 0))

    def cls(arr):
        dil = arr.shape[0]
        return pl.BlockSpec((dil, bm // dil, B_OUT_W), lambda i: (0, i, 0))

    return pl.pallas_call(
        _merge_b_kernel,
        grid=(n // bm,),
        in_specs=[row, row, cls(o1), cls(l1), cls(o2), cls(l2)],
        out_specs=row,
        out_shape=jax.ShapeDtypeStruct((n, B_OUT_W), BF16),
        scratch_shapes=[pltpu.VMEM((B_HEADS_PER_GROUP, bm, HEAD_DIM), F32)] * 4,
        compiler_params=_cparams(("parallel",), VMEM_SMALL_MIB),
        name="merge_b",
    )(o0, l0, o1, l1, o2, l2)


def _branch_kernel(ya_ref, yb_ref, wa_ref, wb_ref, ga_ref, gb_ref, out_ref):
    a = _dot(ya_ref[...], wa_ref[...])
    bb = _dot(yb_ref[...], wb_ref[...])
    merged = ga_ref[...].astype(F32) * a + gb_ref[...].astype(F32) * bb
    out_ref[...] = merged.astype(out_ref.dtype)


def _branch(ya, yb, wa, wb, gates, *, bm, bn):
    n, d = ya.shape[0], wa.shape[1]
    row = lambda w: pl.BlockSpec((bm, w), lambda i, j: (i, 0))
    return pl.pallas_call(
        _branch_kernel,
        grid=(n // bm, d // bn),
        in_specs=[row(A_Q_W), row(B_OUT_W),
                  pl.BlockSpec((A_Q_W, bn), lambda i, j: (0, j)),
                  pl.BlockSpec((B_OUT_W, bn), lambda i, j: (0, j)),
                  pl.BlockSpec((bm, bn), lambda i, j: (i, j)),
                  pl.BlockSpec((bm, bn), lambda i, j: (i, d // bn + j))],
        out_specs=pl.BlockSpec((bm, bn), lambda i, j: (i, j)),
        out_shape=jax.ShapeDtypeStruct((n, d), BF16),
        compiler_params=_cparams(("parallel", "arbitrary"), VMEM_MM_MIB),
        name="branch_proj",
    )(ya, yb, wa, wb, gates, gates)


def _out_kernel(h_ref, a_ref, w_ref, o_ref):
    o_ref[...] = h_ref[...] + _dot(a_ref[...], w_ref[...])


def _out_proj(h, a, w, *, bm, bn):
    n, d = h.shape
    k = a.shape[1]
    return pl.pallas_call(
        _out_kernel,
        grid=(n // bm, d // bn),
        in_specs=[pl.BlockSpec((bm, bn), lambda i, j: (i, j)),
                  pl.BlockSpec((bm, k), lambda i, j: (i, 0)),
                  pl.BlockSpec((k, bn), lambda i, j: (0, j))],
        out_specs=pl.BlockSpec((bm, bn), lambda i, j: (i, j)),
        out_shape=jax.ShapeDtypeStruct((n, d), F32),
        compiler_params=_cparams(("parallel", "arbitrary"), VMEM_WIDE_MIB),
        name="out_proj",
    )(h, a, w)


def _layer(h, g_ffn1, w1_gate, w1_up, w1_down, g_mix, w_in, q_norm_a, k_norm_a,
           w_branch_a, w_branch_b, w_out, g_ffn2, w2_gate, w2_up, w2_down, g_next, *, bsz, last):
    n, d = h.shape
    t = n // bsz
    c = lambda w: w.astype(BF16)
    bm = _tile(n, 512)
    ffn = functools.partial(_ffn, bm=_tile(n, 1024), tf=_tile(w1_gate.shape[1], 256), io_rows=_tile(n, 128))

    later = [w2_gate, w2_up, w2_down, w_in, w_branch_a, w_branch_b, w_out]
    row_tiles, col_steps = n // ffn.keywords["bm"], w1_gate.shape[1] // ffn.keywords["tf"]
    rides = [_cast_block(w.shape, row_tiles, col_steps) is not None for w in later]
    h, u, *cast = ffn(h, g_ffn1, c(w1_gate), c(w1_up), c(w1_down), g_mix, emit_hidden=True,
                      cast_along=[w for w, ok in zip(later, rides) if ok])
    cast = iter(cast)
    *ffn2_w, w_in, w_branch_a, w_branch_b, w_out = [next(cast) if ok else c(w) for w, ok in zip(later, rides)]

    pm = _tile(t, 1024)
    a_w = A_Q_W + 2 * A_KV_W
    qp, kt, va = _proj_a(u, w_in, q_norm_a, k_norm_a, bsz=bsz, bm=pm)
    qkv_b = [_proj_cols(u, w_in, col0=a_w, width=B_GROUP_W, bm=pm, sigmoid=False,
                        name="proj_b0").reshape(bsz, t, B_GROUP_W)]
    for g in range(1, B_N_GROUPS):
        qkv_b.append(_proj_cls(u, w_in, col0=a_w + g * B_GROUP_W, width=B_GROUP_W, bm=pm,
                               dilation=B_PATTERNS[g][1], name=f"proj_b{g}"))
    gates = _proj_cols(u, w_in, col0=a_w + B_QKV_W, width=2 * d, bm=pm, sigmoid=True, name="proj_gates")

    ya = _attn_a(qp.reshape(bsz, t, A_Q_W), kt, va.reshape(bsz, t, A_KV_W), tq=_tile(t, 1024)).reshape(n, A_Q_W)

    slopes = jnp.exp2(-8.0 * jnp.arange(1, B_HEADS + 1, dtype=F32) / B_HEADS)
    o_l = [_attn_b(qkv_b[g], slopes, g, bsz=bsz) for g in range(B_N_GROUPS)]
    o_l[0] = tuple(a.reshape(n, B_OUT_W) for a in o_l[0])

    yb = _merge_b(o_l, bm=bm)
    merged = _branch(ya, yb, w_branch_a, w_branch_b, gates, bm=_tile(n, 1024), bn=_tile(d, 1024))
    h = _out_proj(h, merged, w_out, bm=_tile(n, 1024), bn=_tile(d, 1024))

    if last:
        return ffn(h, g_ffn2, *ffn2_w, g_next, emit_hidden=False, io_rows=_tile(n, 256))
    return ffn(h, g_ffn2, *ffn2_w, g_next, emit_hidden=True)[0]


def kernel(x, g_ffn1, w1_gate, w1_up, w1_down, g_mix, w_in, q_norm_a, k_norm_a, w_branch_a, w_branch_b,
           w_out, g_ffn2, w2_gate, w2_up, w2_down, g_final):
    bsz, t, d = x.shape
    depth = g_ffn1.shape[0]
    h = x.reshape(bsz * t, d)
    for l in range(depth):
        last = l == depth - 1
        h = _layer(h, g_ffn1[l], w1_gate[l], w1_up[l], w1_down[l], g_mix[l], w_in[l], q_norm_a[l],
                   k_norm_a[l], w_branch_a[l], w_branch_b[l], w_out[l], g_ffn2[l], w2_gate[l], w2_up[l],
                   w2_down[l], g_final if last else g_ffn1[l + 1], bsz=bsz, last=last)
    return h.reshape(bsz, t, d)
```

```python
import functools

import jax
import jax.numpy as jnp
from jax import lax
from jax.experimental import pallas as pl
from jax.experimental.pallas import tpu as pltpu

HEAD_DIM = 128
A_HEADS = 16
A_KV_HEADS = 4
A_GROUP = A_HEADS // A_KV_HEADS
B_PATTERNS = ((128, 1), (512, 4), (2048, 16))
B_HEADS_PER_GROUP = 8
B_N_GROUPS = len(B_PATTERNS)
B_HEADS = B_N_GROUPS * B_HEADS_PER_GROUP
A_Q_W = A_HEADS * HEAD_DIM
A_KV_W = A_KV_HEADS * HEAD_DIM
B_OUT_W = B_HEADS_PER_GROUP * HEAD_DIM
B_GROUP_W = 3 * B_OUT_W
B_QKV_W = B_N_GROUPS * B_GROUP_W
GRID_W = 64
ROPE_THETA = 10000.0
ROPE_AXIS_DIM = HEAD_DIM // 2
RMS_EPS = 1e-6
NEG_INF = -1e30

DOWN_CHUNK = 512
FFN_MM_ROWS = 512
ATTN_A_ROWS = 256
PROJ_BN = 2 * A_KV_W
LANES = 128
BF16_SUBLANES = 16
MIB = 1024 * 1024
V7X_VMEM_MIB = 64
VMEM_FFN_MIB = V7X_VMEM_MIB - 4
VMEM_WIDE_MIB = V7X_VMEM_MIB - 8
VMEM_MM_MIB = V7X_VMEM_MIB - 16
VMEM_SMALL_MIB = V7X_VMEM_MIB - 24
BF16 = jnp.bfloat16
F32 = jnp.float32


def _cparams(sem, vmem_mib):
    return pltpu.CompilerParams(dimension_semantics=sem, vmem_limit_bytes=vmem_mib * MIB)


def _tile(n, pref):
    t = min(n, pref)
    while n % t:
        t //= 2
    return t


def _rms(x, gain):
    y = x * lax.rsqrt(jnp.mean(x * x, axis=-1, keepdims=True) + RMS_EPS)
    return y * gain


def _dot(a, b):
    return jnp.dot(a, b, preferred_element_type=F32)


def _ffn_kernel(x_ref, gin_ref, wg_ref, wu_ref, wd_ref, gout_ref, *rest, emit_hidden, n_cast, n_io, n_f):
    cast_in, rest = rest[:n_cast], rest[n_cast:]
    outs, rest = rest[:2 if emit_hidden else 1], rest[2 if emit_hidden else 1:]
    cast_out, (xn_ref, acc_ref) = rest[:n_cast], rest[n_cast:]
    y_ref = outs[-1]
    io_rows = x_ref.shape[0]
    j = pl.program_id(1)
    first_mm, last_mm = n_io - 1, n_io + n_f - 2

    for src_ref, dst_ref in zip(cast_in, cast_out):
        dst_ref[...] = src_ref[...].astype(dst_ref.dtype)

    @pl.when(j <= first_mm)
    def _():
        rows = pl.ds(pl.multiple_of(j * io_rows, io_rows), io_rows)
        x = x_ref[...]
        xn_ref[rows, :] = _rms(x, gin_ref[...]).astype(BF16)
        acc_ref[rows, :] = x

    @pl.when((j >= first_mm) & (j <= last_mm))
    def _():
        for r0 in range(0, acc_ref.shape[0], FFN_MM_ROWS):
            rows = slice(r0, r0 + FFN_MM_ROWS)
            xn = xn_ref[rows, :]
            gate = _dot(xn, wg_ref[...])
            up = _dot(xn, wu_ref[...])
            act = ((0.25 * gate) * (1.0 + jnp.tanh(0.5 * gate)) * up).astype(BF16)
            for c in range(0, acc_ref.shape[1], DOWN_CHUNK):
                acc_ref[rows, c:c + DOWN_CHUNK] += _dot(act, wd_ref[:, c:c + DOWN_CHUNK])

    @pl.when(j >= last_mm)
    def _():
        rows = pl.ds(pl.multiple_of((j - last_mm) * io_rows, io_rows), io_rows)
        h = acc_ref[rows, :]
        if emit_hidden:
            outs[0][...] = h
        y_ref[...] = _rms(h, gout_ref[...]).astype(y_ref.dtype)


def _cast_block(shape, row_tiles, col_steps):
    rows, cols = shape
    if rows % row_tiles or (rows // row_tiles) % BF16_SUBLANES:
        return None
    for width in range(LANES, cols + 1, LANES):
        if cols % width == 0 and cols // width <= col_steps:
            return rows // row_tiles, width
    return None


def _ffn(x, g_in, wg, wu, wd, g_out, *, emit_hidden, bm, tf, io_rows, cast_along=()):
    n, d = x.shape
    f = wg.shape[1]
    n_io, n_f = bm // io_rows, f // tf
    assert bm % FFN_MM_ROWS == 0 and bm % io_rows == 0
    grid = (n // bm, n_f + 2 * (n_io - 1))
    fblk = lambda j: jnp.clip(j - (n_io - 1), 0, n_f - 1)
    vec = pl.BlockSpec((1, d), lambda i, j: (0, 0))
    in_specs = [pl.BlockSpec((io_rows, d), lambda i, j: (i * n_io + jnp.minimum(j, n_io - 1), 0)), vec,
                pl.BlockSpec((d, tf), lambda i, j: (0, fblk(j))),
                pl.BlockSpec((d, tf), lambda i, j: (0, fblk(j))),
                pl.BlockSpec((tf, d), lambda i, j: (fblk(j), 0)),
                vec]
    out_row = pl.BlockSpec((io_rows, d), lambda i, j: (i * n_io + jnp.clip(j - (n_io + n_f - 2), 0, n_io - 1), 0))
    if emit_hidden:
        out_shape = [jax.ShapeDtypeStruct((n, d), F32), jax.ShapeDtypeStruct((n, d), BF16)]
        out_specs = [out_row, out_row]
    else:
        out_shape = [jax.ShapeDtypeStruct((n, d), F32)]
        out_specs = [out_row]
    for w in cast_along:
        blk = _cast_block(w.shape, grid[0], n_f)
        last = w.shape[1] // blk[1] - 1
        spec = pl.BlockSpec(blk, lambda i, j, last=last: (i, jnp.minimum(fblk(j), last)))
        in_specs.append(spec)
        out_specs.append(spec)
        out_shape.append(jax.ShapeDtypeStruct(w.shape, BF16))
    outs = pl.pallas_call(
        functools.partial(_ffn_kernel, emit_hidden=emit_hidden, n_cast=len(cast_along), n_io=n_io, n_f=n_f),
        grid=grid, in_specs=in_specs, out_specs=out_specs, out_shape=out_shape,
        scratch_shapes=[pltpu.VMEM((bm, d), BF16), pltpu.VMEM((bm, d), F32)],
        compiler_params=_cparams(("parallel", "arbitrary"), VMEM_FFN_MIB),
        name="ffn_hidden" if emit_hidden else "ffn_final",
    )(x, g_in.reshape(1, d), wg, wu, wd, g_out.reshape(1, d), *cast_along)
    return outs if len(outs) > 1 else outs[0]


def _rope(y, cos, sin_signed, first_half):
    partner = jnp.where(first_half, pltpu.roll(y, HEAD_DIM - 32, axis=1), pltpu.roll(y, 32, axis=1))
    return y * cos + partner * sin_signed


def _head_pairs(u_ref, w_ref, width):
    u = u_ref[...]
    for c in range(0, width, 2 * HEAD_DIM):
        yield c // HEAD_DIM, _dot(u, w_ref[:, c:c + 2 * HEAD_DIM])


def _proj_q_kernel(u_ref, w_ref, cos_ref, sin_ref, gain_ref, q_ref, acc_ref):
    @pl.when(pl.program_id(0) == 0)
    def _():
        acc_ref[...] = jnp.zeros(acc_ref.shape, F32)

    cos, sin = cos_ref[...], sin_ref[...]
    first_half = (lax.broadcasted_iota(jnp.int32, cos.shape, 1) % 64) < 32
    scale = HEAD_DIM ** -0.5
    for h in range(q_ref.shape[1] // HEAD_DIM):
        lanes = slice(h * HEAD_DIM, (h + 1) * HEAD_DIM)
        y = _rope(_rms(acc_ref[:, lanes], gain_ref[...]), cos, sin, first_half)
        q_ref[:, lanes] = (y * scale).astype(BF16)
    acc_ref[...] = _dot(u_ref[...], w_ref[...])


def _proj_kv_kernel(u_ref, w_ref, cos_ref, sin_ref, gain_ref, kt_ref, v_ref):
    cos, sin = cos_ref[...], sin_ref[...]
    first_half = (lax.broadcasted_iota(jnp.int32, cos.shape, 1) % 64) < 32
    for h0, acc in _head_pairs(u_ref, w_ref, 2 * A_KV_W):
        for h in range(2):
            part = acc[:, h * HEAD_DIM:(h + 1) * HEAD_DIM]
            if h0 + h < A_KV_HEADS:
                y = _rope(_rms(part, gain_ref[...]), cos, sin, first_half)
                kt_ref[0, (h0 + h) * HEAD_DIM:(h0 + h + 1) * HEAD_DIM, :] = y.T.astype(BF16)
            else:
                hv = h0 + h - A_KV_HEADS
                v_ref[:, hv * HEAD_DIM:(hv + 1) * HEAD_DIM] = part.astype(BF16)


def _rope_tables(t):
    rows = t // GRID_W
    row_ids = jnp.repeat(jnp.arange(rows), GRID_W).astype(F32)
    col_ids = jnp.tile(jnp.arange(GRID_W), rows).astype(F32)
    inv = ROPE_THETA ** (-jnp.arange(0, ROPE_AXIS_DIM, 2, dtype=F32) / ROPE_AXIS_DIM)
    ang_r = row_ids[:, None] * inv[None, :]
    ang_c = col_ids[:, None] * inv[None, :]
    cos = jnp.concatenate([jnp.cos(ang_r)] * 2 + [jnp.cos(ang_c)] * 2, axis=-1)
    sin = jnp.concatenate([-jnp.sin(ang_r), jnp.sin(ang_r), -jnp.sin(ang_c), jnp.sin(ang_c)], axis=-1)
    return cos, sin


def _proj_a(u, w, q_gain, k_gain, *, bsz, bm):
    n, d = u.shape
    t = n // bsz
    assert t % bm == 0 and A_Q_W % PROJ_BN == 0
    tiles_per_seq = t // bm
    cos, sin = _rope_tables(t)
    u_spec = pl.BlockSpec((bm, d), lambda i, j: (i, 0))
    tab = pl.BlockSpec((bm, HEAD_DIM), lambda i, j: (i % tiles_per_seq, 0))
    vec = pl.BlockSpec((1, HEAD_DIM), lambda i, j: (0, 0))
    nq = A_Q_W // PROJ_BN
    steps = (n // bm) * nq
    cur = lambda s: jnp.minimum(s, steps - 1)
    fin = lambda s: jnp.maximum(s - 1, 0)
    fin_tab = pl.BlockSpec((bm, HEAD_DIM), lambda s: ((fin(s) // nq) % tiles_per_seq, 0))
    qp = pl.pallas_call(
        _proj_q_kernel,
        grid=(steps + 1,),
        in_specs=[pl.BlockSpec((bm, d), lambda s: (cur(s) // nq, 0)),
                  pl.BlockSpec((d, PROJ_BN), lambda s: (0, cur(s) % nq)),
                  fin_tab, fin_tab, pl.BlockSpec((1, HEAD_DIM), lambda s: (0, 0))],
        out_specs=pl.BlockSpec((bm, PROJ_BN), lambda s: (fin(s) // nq, fin(s) % nq)),
        out_shape=jax.ShapeDtypeStruct((n, A_Q_W), BF16),
        scratch_shapes=[pltpu.VMEM((bm, PROJ_BN), F32)],
        compiler_params=_cparams(("arbitrary",), VMEM_WIDE_MIB),
        name="proj_q",
    )(u, w, cos, sin, q_gain.reshape(1, HEAD_DIM))
    kt, v = pl.pallas_call(
        _proj_kv_kernel,
        grid=(n // bm, 1),
        in_specs=[u_spec, pl.BlockSpec((d, 2 * A_KV_W), lambda i, j: (0, A_Q_W // (2 * A_KV_W))), tab, tab, vec],
        out_specs=(pl.BlockSpec((1, A_KV_W, bm), lambda i, j: (i // tiles_per_seq, 0, i % tiles_per_seq)),
                   pl.BlockSpec((bm, A_KV_W), lambda i, j: (i, 0))),
        out_shape=(jax.ShapeDtypeStruct((bsz, A_KV_W, t), BF16), jax.ShapeDtypeStruct((n, A_KV_W), BF16)),
        compiler_params=_cparams(("parallel", "arbitrary"), VMEM_MM_MIB),
        name="proj_kv",
    )(u, w, cos, sin, k_gain.reshape(1, HEAD_DIM))
    return qp, kt, v


def _proj_cols_kernel(u_ref, w_ref, o_ref, *, sigmoid):
    acc = _dot(u_ref[...], w_ref[...])
    if sigmoid:
        acc = 0.5 * jnp.tanh(0.5 * acc) + 0.5
    o_ref[...] = acc.astype(o_ref.dtype)


def _proj_cols(u, w, *, col0, width, bm, sigmoid, name):
    n, d = u.shape
    bn = _tile(width, PROJ_BN)
    assert col0 % bn == 0
    return pl.pallas_call(
        functools.partial(_proj_cols_kernel, sigmoid=sigmoid),
        grid=(n // bm, width // bn),
        in_specs=[pl.BlockSpec((bm, d), lambda i, j: (i, 0)),
                  pl.BlockSpec((d, bn), lambda i, j: (0, col0 // bn + j))],
        out_specs=pl.BlockSpec((bm, bn), lambda i, j: (i, j)),
        out_shape=jax.ShapeDtypeStruct((n, width), BF16),
        compiler_params=_cparams(("parallel", "arbitrary"), VMEM_MM_MIB),
        name=name,
    )(u, w)


def _proj_cls_kernel(u_ref, w_ref, o_ref, acc_ref, *, dilation):
    acc = _dot(u_ref[...], w_ref[...])
    rows = o_ref.shape[1]
    for c in range(acc_ref.shape[0]):
        lanes = slice(c * HEAD_DIM, (c + 1) * HEAD_DIM)
        acc_ref[c] = acc[:, lanes]
        for r in range(dilation):
            o_ref[r, :, lanes] = acc_ref[c, pl.ds(r, rows, stride=dilation), :].astype(o_ref.dtype)


def _proj_cls(u, w, *, col0, width, bm, dilation, name):
    n, d = u.shape
    assert col0 % PROJ_BN == 0
    return pl.pallas_call(
        functools.partial(_proj_cls_kernel, dilation=dilation),
        grid=(n // bm, width // PROJ_BN),
        in_specs=[pl.BlockSpec((bm, d), lambda i, j: (i, 0)),
                  pl.BlockSpec((d, PROJ_BN), lambda i, j: (0, col0 // PROJ_BN + j))],
        out_specs=pl.BlockSpec((dilation, bm // dilation, PROJ_BN), lambda i, j: (0, i, j)),
        out_shape=jax.ShapeDtypeStruct((dilation, n // dilation, width), BF16),
        scratch_shapes=[pltpu.VMEM((PROJ_BN // HEAD_DIM, bm, HEAD_DIM), F32)],
        compiler_params=_cparams(("parallel", "arbitrary"), VMEM_WIDE_MIB),
        name=name,
    )(u, w)


def _attn_a_kernel(q_ref, kt_ref, v_ref, o_ref, v1_ref):
    @pl.when(pl.program_id(2) == 0)
    def _():
        v1_ref[:, :HEAD_DIM] = v_ref[0]
        v1_ref[:, HEAD_DIM:] = jnp.ones((v1_ref.shape[0], HEAD_DIM), BF16)

    kt = kt_ref[0]
    v1 = v1_ref[...]
    for r0 in range(0, q_ref.shape[1], ATTN_A_ROWS):
        rows = slice(r0, r0 + ATTN_A_ROWS)
        for g in range(A_GROUP):
            sl = slice(g * HEAD_DIM, (g + 1) * HEAD_DIM)
            s = _dot(q_ref[0, rows, sl], kt)
            p = jnp.exp(s - jnp.max(s, axis=-1, keepdims=True)).astype(BF16)
            ov = _dot(p, v1)
            o_ref[0, rows, sl] = (ov[:, :HEAD_DIM] / ov[:, HEAD_DIM:]).astype(o_ref.dtype)


def _attn_a(qp, kt, v, *, tq):
    b, t, _ = qp.shape
    gw = A_GROUP * HEAD_DIM
    assert tq % ATTN_A_ROWS == 0
    return pl.pallas_call(
        _attn_a_kernel,
        grid=(b, A_KV_HEADS, t // tq),
        in_specs=[pl.BlockSpec((1, tq, gw), lambda bi, kv, i: (bi, i, kv)),
                  pl.BlockSpec((1, HEAD_DIM, t), lambda bi, kv, i: (bi, kv, 0)),
                  pl.BlockSpec((1, t, HEAD_DIM), lambda bi, kv, i: (bi, 0, kv))],
        out_specs=pl.BlockSpec((1, tq, gw), lambda bi, kv, i: (bi, i, kv)),
        out_shape=jax.ShapeDtypeStruct((b, t, A_Q_W), BF16),
        scratch_shapes=[pltpu.VMEM((t, 2 * HEAD_DIM), BF16)],
        compiler_params=_cparams(("parallel", "parallel", "arbitrary"), VMEM_MM_MIB),
        name="attn_a",
    )(qp, kt, v)


def _attn_b_kernel(slopes_ref, q_ref, k_ref, v_ref, o_ref, lse_ref, *, group, dilation, half, tq, kw,
                   heads, interleave):
    length = q_ref.shape[0]
    head0 = group * B_HEADS_PER_GROUP + pl.program_id(1) * heads
    scale = HEAD_DIM ** -0.5
    row = lax.broadcasted_iota(jnp.int32, (tq, kw), 0)
    col = lax.broadcasted_iota(jnp.int32, (tq, kw), 1)
    ones = jnp.ones((kw, HEAD_DIM), BF16)

    def tile(i, lanes, slope):
        m0 = pl.multiple_of(i * tq, tq)
        ks = pl.multiple_of(jnp.clip(m0 - half, 0, length - kw), half)
        q = q_ref[pl.ds(m0, tq), lanes]
        k = k_ref[pl.ds(ks, kw), lanes]
        v = v_ref[pl.ds(ks, kw), lanes]
        s = lax.dot_general(q, k, (((1,), (1,)), ((), ())), preferred_element_type=F32) * scale
        dist = jnp.abs(col - row + (ks - m0))
        bias = -slope * (dist * dilation).astype(F32)
        s = jnp.where(dist <= half, s + bias, NEG_INF)
        m = jnp.max(s, axis=-1, keepdims=True)
        p = jnp.exp(s - m).astype(BF16)
        ov = _dot(p, jnp.concatenate([v, ones], axis=1))
        denom = ov[:, HEAD_DIM:]
        o_ref[pl.ds(m0, tq), lanes] = (ov[:, :HEAD_DIM] / denom).astype(o_ref.dtype)
        lse_ref[pl.ds(m0, tq), lanes] = m + jnp.log(denom)

    for hh in range(heads):
        lanes = slice(hh * HEAD_DIM, (hh + 1) * HEAD_DIM)
        slope = slopes_ref[head0 + hh]

        def body(it, carry, lanes=lanes, slope=slope):
            for u in range(interleave):
                tile(it * interleave + u, lanes, slope)
            return carry

        lax.fori_loop(0, length // (tq * interleave), body, 0)


def _attn_b(qkv, slopes, group, *, bsz):
    window, dilation = B_PATTERNS[group]
    length = qkv.shape[-2] // (1 if dilation == 1 else bsz)
    half = (window // 2) // dilation
    tq = min(128, length)
    kw = min(length, tq + 2 * half)
    heads = 1 if dilation == 1 else B_HEADS_PER_GROUP
    steps = B_HEADS_PER_GROUP // heads
    width = heads * HEAD_DIM

    if dilation == 1:
        in_spec = lambda part: pl.BlockSpec((None, length, width), lambda bi, h, r, s: (bi, 0, part * steps + h))
        out_spec = pl.BlockSpec((None, length, width), lambda bi, h, r, s: (bi, 0, h))
        out_dims = (bsz, length, B_OUT_W)
    else:
        in_spec = lambda part: pl.BlockSpec((None, length, width), lambda bi, h, r, s: (r, bi, part * steps + h))
        out_spec = pl.BlockSpec((None, length, width), lambda bi, h, r, s: (r, bi, h))
        out_dims = (dilation, bsz * length, B_OUT_W)
    n_tiles = length // tq
    return pl.pallas_call(
        functools.partial(_attn_b_kernel, group=group, dilation=dilation, half=half, tq=tq, kw=kw,
                          heads=heads, interleave=min(8, n_tiles)),
        grid_spec=pltpu.PrefetchScalarGridSpec(
            num_scalar_prefetch=1,
            grid=(bsz, steps, dilation),
            in_specs=[in_spec(0), in_spec(1), in_spec(2)],
            out_specs=(out_spec, out_spec)),
        out_shape=(jax.ShapeDtypeStruct(out_dims, BF16), jax.ShapeDtypeStruct(out_dims, F32)),
        compiler_params=_cparams(("parallel", "parallel", "parallel"), VMEM_SMALL_MIB),
        name=f"attn_b{group}",
    )(slopes, qkv, qkv, qkv)


def _merge_b_kernel(o0_ref, l0_ref, o1_ref, l1_ref, o2_ref, l2_ref, yb_ref, on1_ref, ln1_ref, on2_ref, ln2_ref):
    for o_ref, l_ref, on_ref, ln_ref in ((o1_ref, l1_ref, on1_ref, ln1_ref),
                                         (o2_ref, l2_ref, on2_ref, ln2_ref)):
        dilation, rows = o_ref.shape[0], o_ref.shape[1]
        for c in range(on_ref.shape[0]):
            lanes = slice(c * HEAD_DIM, (c + 1) * HEAD_DIM)
            for r in range(dilation):
                on_ref[c, pl.ds(r, rows, stride=dilation), :] = o_ref[r, :, lanes].astype(F32)
                ln_ref[c, pl.ds(r, rows, stride=dilation), :] = l_ref[r, :, lanes]
    for c in range(on1_ref.shape[0]):
        lanes = slice(c * HEAD_DIM, (c + 1) * HEAD_DIM)
        l0, l1, l2 = l0_ref[:, lanes], ln1_ref[c], ln2_ref[c]
        m = jnp.maximum(jnp.maximum(l0, l1), l2)
        e0, e1, e2 = jnp.exp(l0 - m), jnp.exp(l1 - m), jnp.exp(l2 - m)
        tot = e0 + e1 + e2
        yb = (e0 / tot) * o0_ref[:, lanes].astype(F32) + (e1 / tot) * on1_ref[c] + (e2 / tot) * on2_ref[c]
        yb_ref[:, lanes] = yb.astype(BF16)


def _merge_b(o_l, *, bm):
    (o0, l0), (o1, l1), (o2, l2) = o_l
    n = o0.shape[0]
    row = pl.BlockSpec((bm, B_OUT_W), lambda i: (i, 0))

    def cls(arr):
        dil = arr.shape[0]
        return pl.BlockSpec((dil, bm // dil, B_OUT_W), lambda i: (0, i, 0))

    return pl.pallas_call(
        _merge_b_kernel,
        grid=(n // bm,),
        in_specs=[row, row, cls(o1), cls(l1), cls(o2), cls(l2)],
        out_specs=row,
        out_shape=jax.ShapeDtypeStruct((n, B_OUT_W), BF16),
        scratch_shapes=[pltpu.VMEM((B_HEADS_PER_GROUP, bm, HEAD_DIM), F32)] * 4,
        compiler_params=_cparams(("parallel",), VMEM_SMALL_MIB),
        name="merge_b",
    )(o0, l0, o1, l1, o2, l2)


def _branch_kernel(ya_ref, yb_ref, wa_ref, wb_ref, ga_ref, gb_ref, out_ref):
    a = _dot(ya_ref[...], wa_ref[...])
    bb = _dot(yb_ref[...], wb_ref[...])
    merged = ga_ref[...].astype(F32) * a + gb_ref[...].astype(F32) * bb
    out_ref[...] = merged.astype(out_ref.dtype)


def _branch(ya, yb, wa, wb, gates, *, bm, bn):
    n, d = ya.shape[0], wa.shape[1]
    row = lambda w: pl.BlockSpec((bm, w), lambda i, j: (i, 0))
    return pl.pallas_call(
        _branch_kernel,
        grid=(n // bm, d // bn),
        in_specs=[row(A_Q_W), row(B_OUT_W),
                  pl.BlockSpec((A_Q_W, bn), lambda i, j: (0, j)),
                  pl.BlockSpec((B_OUT_W, bn), lambda i, j: (0, j)),
                  pl.BlockSpec((bm, bn), lambda i, j: (i, j)),
                  pl.BlockSpec((bm, bn), lambda i, j: (i, d // bn + j))],
        out_specs=pl.BlockSpec((bm, bn), lambda i, j: (i, j)),
        out_shape=jax.ShapeDtypeStruct((n, d), BF16),
        compiler_params=_cparams(("parallel", "arbitrary"), VMEM_MM_MIB),
        name="branch_proj",
    )(ya, yb, wa, wb, gates, gates)


def _out_kernel(h_ref, a_ref, w_ref, o_ref):
    o_ref[...] = h_ref[...] + _dot(a_ref[...], w_ref[...])


def _out_proj(h, a, w, *, bm, bn):
    n, d = h.shape
    k = a.shape[1]
    return pl.pallas_call(
        _out_kernel,
        grid=(n // bm, d // bn),
        in_specs=[pl.BlockSpec((bm, bn), lambda i, j: (i, j)),
                  pl.BlockSpec((bm, k), lambda i, j: (i, 0)),
                  pl.BlockSpec((k, bn), lambda i, j: (0, j))],
        out_specs=pl.BlockSpec((bm, bn), lambda i, j: (i, j)),
        out_shape=jax.ShapeDtypeStruct((n, d), F32),
        compiler_params=_cparams(("parallel", "arbitrary"), VMEM_WIDE_MIB),
        name="out_proj",
    )(h, a, w)


def _layer(h, g_ffn1, w1_gate, w1_up, w1_down, g_mix, w_in, q_norm_a, k_norm_a,
           w_branch_a, w_branch_b, w_out, g_ffn2, w2_gate, w2_up, w2_down, g_next, *, bsz, last):
    n, d = h.shape
    t = n // bsz
    c = lambda w: w.astype(BF16)
    bm = _tile(n, 512)
    ffn = functools.partial(_ffn, bm=_tile(n, 1024), tf=_tile(w1_gate.shape[1], 256), io_rows=_tile(n, 128))

    later = [w2_gate, w2_up, w2_down, w_in, w_branch_a, w_branch_b, w_out]
    row_tiles, col_steps = n // ffn.keywords["bm"], w1_gate.shape[1] // ffn.keywords["tf"]
    rides = [_cast_block(w.shape, row_tiles, col_steps) is not None for w in later]
    h, u, *cast = ffn(h, g_ffn1, c(w1_gate), c(w1_up), c(w1_down), g_mix, emit_hidden=True,
                      cast_along=[w for w, ok in zip(later, rides) if ok])
    cast = iter(cast)
    *ffn2_w, w_in, w_branch_a, w_branch_b, w_out = [next(cast) if ok else c(w) for w, ok in zip(later, rides)]

    pm = _tile(t, 1024)
    a_w = A_Q_W + 2 * A_KV_W
    qp, kt, va = _proj_a(u, w_in, q_norm_a, k_norm_a, bsz=bsz, bm=pm)
    qkv_b = [_proj_cols(u, w_in, col0=a_w, width=B_GROUP_W, bm=pm, sigmoid=False,
                        name="proj_b0").reshape(bsz, t, B_GROUP_W)]
    for g in range(1, B_N_GROUPS):
        qkv_b.append(_proj_cls(u, w_in, col0=a_w + g * B_GROUP_W, width=B_GROUP_W, bm=pm,
                               dilation=B_PATTERNS[g][1], name=f"proj_b{g}"))
    gates = _proj_cols(u, w_in, col0=a_w + B_QKV_W, width=2 * d, bm=pm, sigmoid=True, name="proj_gates")

    ya = _attn_a(qp.reshape(bsz, t, A_Q_W), kt, va.reshape(bsz, t, A_KV_W), tq=_tile(t, 1024)).reshape(n, A_Q_W)

    slopes = jnp.exp2(-8.0 * jnp.arange(1, B_HEADS + 1, dtype=F32) / B_HEADS)
    o_l = [_attn_b(qkv_b[g], slopes, g, bsz=bsz) for g in range(B_N_GROUPS)]
    o_l[0] = tuple(a.reshape(n, B_OUT_W) for a in o_l[0])

    yb = _merge_b(o_l, bm=bm)
    merged = _branch(ya, yb, w_branch_a, w_branch_b, gates, bm=_tile(n, 1024), bn=_tile(d, 1024))
    h = _out_proj(h, merged, w_out, bm=_tile(n, 1024), bn=_tile(d, 1024))

    if last:
        return ffn(h, g_ffn2, *ffn2_w, g_next, emit_hidden=False, io_rows=_tile(n, 256))
    return ffn(h, g_ffn2, *ffn2_w, g_next, emit_hidden=True)[0]


def kernel(x, g_ffn1, w1_gate, w1_up, w1_down, g_mix, w_in, q_norm_a, k_norm_a, w_branch_a, w_branch_b,
           w_out, g_ffn2, w2_gate, w2_up, w2_down, g_final):
    bsz, t, d = x.shape
    depth = g_ffn1.shape[0]
    h = x.reshape(bsz * t, d)
    for l in range(depth):
        last = l == depth - 1
        h = _layer(h, g_ffn1[l], w1_gate[l], w1_up[l], w1_down[l], g_mix[l], w_in[l], q_norm_a[l],
                   k_norm_a[l], w_branch_a[l], w_branch_b[l], w_out[l], g_ffn2[l], w2_gate[l], w2_up[l],
                   w2_down[l], g_final if last else g_ffn1[l + 1], bsz=bsz, last=last)
    return h.reshape(bsz, t, d)
```

```python
import functools

import jax
import jax.numpy as jnp
from jax import lax
from jax.experimental import pallas as pl
from jax.experimental.pallas import tpu as pltpu

HEAD_DIM = 128
A_HEADS = 16
A_KV_HEADS = 4
A_GROUP = A_HEADS // A_KV_HEADS
B_PATTERNS = ((128, 1), (512, 4), (2048, 16))
B_HEADS_PER_GROUP = 8
B_N_GROUPS = len(B_PATTERNS)
B_HEADS = B_N_GROUPS * B_HEADS_PER_GROUP
A_Q_W = A_HEADS * HEAD_DIM
A_KV_W = A_KV_HEADS * HEAD_DIM
B_OUT_W = B_HEADS_PER_GROUP * HEAD_DIM
B_GROUP_W = 3 * B_OUT_W
B_QKV_W = B_N_GROUPS * B_GROUP_W
GRID_W = 64
ROPE_THETA = 10000.0
ROPE_AXIS_DIM = HEAD_DIM // 2
RMS_EPS = 1e-6
NEG_INF = -1e30

DOWN_CHUNK = 512
FFN_MM_ROWS = 512
ATTN_A_ROWS = 128
PROJ_BN = 2 * A_KV_W
LANES = 128
BF16_SUBLANES = 16
MIB = 1024 * 1024
V7X_VMEM_MIB = 64
VMEM_FFN_MIB = V7X_VMEM_MIB - 4
VMEM_WIDE_MIB = V7X_VMEM_MIB - 8
VMEM_MM_MIB = V7X_VMEM_MIB - 16
VMEM_SMALL_MIB = V7X_VMEM_MIB - 24
BF16 = jnp.bfloat16
F32 = jnp.float32


def _cparams(sem, vmem_mib):
    return pltpu.CompilerParams(dimension_semantics=sem, vmem_limit_bytes=vmem_mib * MIB)


def _tile(n, pref):
    t = min(n, pref)
    while n % t:
        t //= 2
    return t


def _rms(x, gain):
    y = x * lax.rsqrt(jnp.mean(x * x, axis=-1, keepdims=True) + RMS_EPS)
    return y * gain


def _dot(a, b):
    return jnp.dot(a, b, preferred_element_type=F32)


def _ffn_kernel(x_ref, gin_ref, wg_ref, wu_ref, wd_ref, gout_ref, *rest, emit_hidden, n_cast, n_io, n_f):
    cast_in, rest = rest[:n_cast], rest[n_cast:]
    outs, rest = rest[:2 if emit_hidden else 1], rest[2 if emit_hidden else 1:]
    cast_out, (xn_ref, acc_ref) = rest[:n_cast], rest[n_cast:]
    y_ref = outs[-1]
    io_rows = x_ref.shape[0]
    j = pl.program_id(1)
    first_mm, last_mm = n_io - 1, n_io + n_f - 2

    for src_ref, dst_ref in zip(cast_in, cast_out):
        dst_ref[...] = src_ref[...].astype(dst_ref.dtype)

    @pl.when(j <= first_mm)
    def _():
        rows = pl.ds(pl.multiple_of(j * io_rows, io_rows), io_rows)
        x = x_ref[...]
        xn_ref[rows, :] = _rms(x, gin_ref[...]).astype(BF16)
        acc_ref[rows, :] = x

    @pl.when((j >= first_mm) & (j <= last_mm))
    def _():
        for r0 in range(0, acc_ref.shape[0], FFN_MM_ROWS):
            rows = slice(r0, r0 + FFN_MM_ROWS)
            xn = xn_ref[rows, :]
            gate = _dot(xn, wg_ref[...])
            up = _dot(xn, wu_ref[...])
            act = ((0.25 * gate) * (1.0 + jnp.tanh(0.5 * gate)) * up).astype(BF16)
            for c in range(0, acc_ref.shape[1], DOWN_CHUNK):
                acc_ref[rows, c:c + DOWN_CHUNK] += _dot(act, wd_ref[:, c:c + DOWN_CHUNK])

    @pl.when(j >= last_mm)
    def _():
        rows = pl.ds(pl.multiple_of((j - last_mm) * io_rows, io_rows), io_rows)
        h = acc_ref[rows, :]
        if emit_hidden:
            outs[0][...] = h
        y_ref[...] = _rms(h, gout_ref[...]).astype(y_ref.dtype)


def _cast_block(shape, row_tiles, col_steps):
    rows, cols = shape
    if rows % row_tiles or (rows // row_tiles) % BF16_SUBLANES:
        return None
    for width in range(LANES, cols + 1, LANES):
        if cols % width == 0 and cols // width <= col_steps:
            return rows // row_tiles, width
    return None


def _ffn(x, g_in, wg, wu, wd, g_out, *, emit_hidden, bm, tf, io_rows, cast_along=()):
    n, d = x.shape
    f = wg.shape[1]
    n_io, n_f = bm // io_rows, f // tf
    assert bm % FFN_MM_ROWS == 0 and bm % io_rows == 0
    grid = (n // bm, n_f + 2 * (n_io - 1))
    fblk = lambda j: jnp.clip(j - (n_io - 1), 0, n_f - 1)
    vec = pl.BlockSpec((1, d), lambda i, j: (0, 0))
    in_specs = [pl.BlockSpec((io_rows, d), lambda i, j: (i * n_io + jnp.minimum(j, n_io - 1), 0)), vec,
                pl.BlockSpec((d, tf), lambda i, j: (0, fblk(j))),
                pl.BlockSpec((d, tf), lambda i, j: (0, fblk(j))),
                pl.BlockSpec((tf, d), lambda i, j: (fblk(j), 0)),
                vec]
    out_row = pl.BlockSpec((io_rows, d), lambda i, j: (i * n_io + jnp.clip(j - (n_io + n_f - 2), 0, n_io - 1), 0))
    if emit_hidden:
        out_shape = [jax.ShapeDtypeStruct((n, d), F32), jax.ShapeDtypeStruct((n, d), BF16)]
        out_specs = [out_row, out_row]
    else:
        out_shape = [jax.ShapeDtypeStruct((n, d), F32)]
        out_specs = [out_row]
    for w in cast_along:
        blk = _cast_block(w.shape, grid[0], n_f)
        last = w.shape[1] // blk[1] - 1
        spec = pl.BlockSpec(blk, lambda i, j, last=last: (i, jnp.minimum(fblk(j), last)))
        in_specs.append(spec)
        out_specs.append(spec)
        out_shape.append(jax.ShapeDtypeStruct(w.shape, BF16))
    outs = pl.pallas_call(
        functools.partial(_ffn_kernel, emit_hidden=emit_hidden, n_cast=len(cast_along), n_io=n_io, n_f=n_f),
        grid=grid, in_specs=in_specs, out_specs=out_specs, out_shape=out_shape,
        scratch_shapes=[pltpu.VMEM((bm, d), BF16), pltpu.VMEM((bm, d), F32)],
        compiler_params=_cparams(("parallel", "arbitrary"), VMEM_FFN_MIB),
        name="ffn_hidden" if emit_hidden else "ffn_final",
    )(x, g_in.reshape(1, d), wg, wu, wd, g_out.reshape(1, d), *cast_along)
    return outs if len(outs) > 1 else outs[0]


def _rope(y, cos, sin_signed, first_half):
    partner = jnp.where(first_half, pltpu.roll(y, HEAD_DIM - 32, axis=1), pltpu.roll(y, 32, axis=1))
    return y * cos + partner * sin_signed


def _head_pairs(u_ref, w_ref, width):
    u = u_ref[...]
    for c in range(0, width, 2 * HEAD_DIM):
        yield c // HEAD_DIM, _dot(u, w_ref[:, c:c + 2 * HEAD_DIM])


def _proj_q_kernel(u_ref, w_ref, cos_ref, sin_ref, gain_ref, q_ref, acc_ref):
    @pl.when(pl.program_id(0) == 0)
    def _():
        acc_ref[...] = jnp.zeros(acc_ref.shape, F32)

    cos, sin = cos_ref[...], sin_ref[...]
    first_half = (lax.broadcasted_iota(jnp.int32, cos.shape, 1) % 64) < 32
    scale = HEAD_DIM ** -0.5
    for h in range(q_ref.shape[1] // HEAD_DIM):
        lanes = slice(h * HEAD_DIM, (h + 1) * HEAD_DIM)
        y = _rope(_rms(acc_ref[:, lanes], gain_ref[...]), cos, sin, first_half)
        q_ref[:, lanes] = (y * scale).astype(BF16)
    acc_ref[...] = _dot(u_ref[...], w_ref[...])


def _proj_kv_kernel(u_ref, w_ref, cos_ref, sin_ref, gain_ref, kt_ref, v_ref):
    cos, sin = cos_ref[...], sin_ref[...]
    first_half = (lax.broadcasted_iota(jnp.int32, cos.shape, 1) % 64) < 32
    for h0, acc in _head_pairs(u_ref, w_ref, 2 * A_KV_W):
        for h in range(2):
            part = acc[:, h * HEAD_DIM:(h + 1) * HEAD_DIM]
            if h0 + h < A_KV_HEADS:
                y = _rope(_rms(part, gain_ref[...]), cos, sin, first_half)
                kt_ref[0, (h0 + h) * HEAD_DIM:(h0 + h + 1) * HEAD_DIM, :] = y.T.astype(BF16)
            else:
                hv = h0 + h - A_KV_HEADS
                v_ref[:, hv * HEAD_DIM:(hv + 1) * HEAD_DIM] = part.astype(BF16)


def _rope_tables(t):
    rows = t // GRID_W
    row_ids = jnp.repeat(jnp.arange(rows), GRID_W).astype(F32)
    col_ids = jnp.tile(jnp.arange(GRID_W), rows).astype(F32)
    inv = ROPE_THETA ** (-jnp.arange(0, ROPE_AXIS_DIM, 2, dtype=F32) / ROPE_AXIS_DIM)
    ang_r = row_ids[:, None] * inv[None, :]
    ang_c = col_ids[:, None] * inv[None, :]
    cos = jnp.concatenate([jnp.cos(ang_r)] * 2 + [jnp.cos(ang_c)] * 2, axis=-1)
    sin = jnp.concatenate([-jnp.sin(ang_r), jnp.sin(ang_r), -jnp.sin(ang_c), jnp.sin(ang_c)], axis=-1)
    return cos, sin


def _proj_a(u, w, q_gain, k_gain, *, bsz, bm):
    n, d = u.shape
    t = n // bsz
    assert t % bm == 0 and A_Q_W % PROJ_BN == 0
    tiles_per_seq = t // bm
    cos, sin = _rope_tables(t)
    u_spec = pl.BlockSpec((bm, d), lambda i, j: (i, 0))
    tab = pl.BlockSpec((bm, HEAD_DIM), lambda i, j: (i % tiles_per_seq, 0))
    vec = pl.BlockSpec((1, HEAD_DIM), lambda i, j: (0, 0))
    nq = A_Q_W // PROJ_BN
    steps = (n // bm) * nq
    cur = lambda s: jnp.minimum(s, steps - 1)
    fin = lambda s: jnp.maximum(s - 1, 0)
    fin_tab = pl.BlockSpec((bm, HEAD_DIM), lambda s: ((fin(s) // nq) % tiles_per_seq, 0))
    qp = pl.pallas_call(
        _proj_q_kernel,
        grid=(steps + 1,),
        in_specs=[pl.BlockSpec((bm, d), lambda s: (cur(s) // nq, 0)),
                  pl.BlockSpec((d, PROJ_BN), lambda s: (0, cur(s) % nq)),
                  fin_tab, fin_tab, pl.BlockSpec((1, HEAD_DIM), lambda s: (0, 0))],
        out_specs=pl.BlockSpec((bm, PROJ_BN), lambda s: (fin(s) // nq, fin(s) % nq)),
        out_shape=jax.ShapeDtypeStruct((n, A_Q_W), BF16),
        scratch_shapes=[pltpu.VMEM((bm, PROJ_BN), F32)],
        compiler_params=_cparams(("arbitrary",), VMEM_WIDE_MIB),
        name="proj_q",
    )(u, w, cos, sin, q_gain.reshape(1, HEAD_DIM))
    kt, v = pl.pallas_call(
        _proj_kv_kernel,
        grid=(n // bm, 1),
        in_specs=[u_spec, pl.BlockSpec((d, 2 * A_KV_W), lambda i, j: (0, A_Q_W // (2 * A_KV_W))), tab, tab, vec],
        out_specs=(pl.BlockSpec((1, A_KV_W, bm), lambda i, j: (i // tiles_per_seq, 0, i % tiles_per_seq)),
                   pl.BlockSpec((bm, A_KV_W), lambda i, j: (i, 0))),
        out_shape=(jax.ShapeDtypeStruct((bsz, A_KV_W, t), BF16), jax.ShapeDtypeStruct((n, A_KV_W), BF16)),
        compiler_params=_cparams(("parallel", "arbitrary"), VMEM_MM_MIB),
        name="proj_kv",
    )(u, w, cos, sin, k_gain.reshape(1, HEAD_DIM))
    return qp, kt, v


def _proj_cols_kernel(u_ref, w_ref, o_ref, *, sigmoid):
    acc = _dot(u_ref[...], w_ref[...])
    if sigmoid:
        acc = 0.5 * jnp.tanh(0.5 * acc) + 0.5
    o_ref[...] = acc.astype(o_ref.dtype)


def _proj_cols(u, w, *, col0, width, bm, sigmoid, name):
    n, d = u.shape
    bn = _tile(width, PROJ_BN)
    assert col0 % bn == 0
    return pl.pallas_call(
        functools.partial(_proj_cols_kernel, sigmoid=sigmoid),
        grid=(n // bm, width // bn),
        in_specs=[pl.BlockSpec((bm, d), lambda i, j: (i, 0)),
                  pl.BlockSpec((d, bn), lambda i, j: (0, col0 // bn + j))],
        out_specs=pl.BlockSpec((bm, bn), lambda i, j: (i, j)),
        out_shape=jax.ShapeDtypeStruct((n, width), BF16),
        compiler_params=_cparams(("parallel", "arbitrary"), VMEM_MM_MIB),
        name=name,
    )(u, w)


def _proj_cls_kernel(u_ref, w_ref, o_ref, acc_ref, *, dilation):
    acc = _dot(u_ref[...], w_ref[...])
    rows = o_ref.shape[1]
    for c in range(acc_ref.shape[0]):
        lanes = slice(c * HEAD_DIM, (c + 1) * HEAD_DIM)
        acc_ref[c] = acc[:, lanes]
        for r in range(dilation):
            o_ref[r, :, lanes] = acc_ref[c, pl.ds(r, rows, stride=dilation), :].astype(o_ref.dtype)


def _proj_cls(u, w, *, col0, width, bm, dilation, name):
    n, d = u.shape
    assert col0 % PROJ_BN == 0
    return pl.pallas_call(
        functools.partial(_proj_cls_kernel, dilation=dilation),
        grid=(n // bm, width // PROJ_BN),
        in_specs=[pl.BlockSpec((bm, d), lambda i, j: (i, 0)),
                  pl.BlockSpec((d, PROJ_BN), lambda i, j: (0, col0 // PROJ_BN + j))],
        out_specs=pl.BlockSpec((dilation, bm // dilation, PROJ_BN), lambda i, j: (0, i, j)),
        out_shape=jax.ShapeDtypeStruct((dilation, n // dilation, width), BF16),
        scratch_shapes=[pltpu.VMEM((PROJ_BN // HEAD_DIM, bm, HEAD_DIM), F32)],
        compiler_params=_cparams(("parallel", "arbitrary"), VMEM_WIDE_MIB),
        name=name,
    )(u, w)


def _attn_a_kernel(q_ref, kt_ref, v_ref, o_ref, v1_ref):
    @pl.when(pl.program_id(2) == 0)
    def _():
        v1_ref[:, :HEAD_DIM] = v_ref[0]
        v1_ref[:, HEAD_DIM:] = jnp.ones((v1_ref.shape[0], HEAD_DIM), BF16)

    kt = kt_ref[0]
    v1 = v1_ref[...]
    for r0 in range(0, q_ref.shape[1], ATTN_A_ROWS):
        rows = slice(r0, r0 + ATTN_A_ROWS)
        for g in range(A_GROUP):
            sl = slice(g * HEAD_DIM, (g + 1) * HEAD_DIM)
            s = _dot(q_ref[0, rows, sl], kt)
            p = jnp.exp(s - jnp.max(s, axis=-1, keepdims=True)).astype(BF16)
            ov = _dot(p, v1)
            o_ref[0, rows, sl] = (ov[:, :HEAD_DIM] / ov[:, HEAD_DIM:]).astype(o_ref.dtype)


def _attn_a(qp, kt, v, *, tq):
    b, t, _ = qp.shape
    gw = A_GROUP * HEAD_DIM
    assert tq % ATTN_A_ROWS == 0
    return pl.pallas_call(
        _attn_a_kernel,
        grid=(b, A_KV_HEADS, t // tq),
        in_specs=[pl.BlockSpec((1, tq, gw), lambda bi, kv, i: (bi, i, kv)),
                  pl.BlockSpec((1, HEAD_DIM, t), lambda bi, kv, i: (bi, kv, 0)),
                  pl.BlockSpec((1, t, HEAD_DIM), lambda bi, kv, i: (bi, 0, kv))],
        out_specs=pl.BlockSpec((1, tq, gw), lambda bi, kv, i: (bi, i, kv)),
        out_shape=jax.ShapeDtypeStruct((b, t, A_Q_W), BF16),
        scratch_shapes=[pltpu.VMEM((t, 2 * HEAD_DIM), BF16)],
        compiler_params=_cparams(("parallel", "parallel", "arbitrary"), VMEM_MM_MIB),
        name="attn_a",
    )(qp, kt, v)


def _attn_b_kernel(slopes_ref, q_ref, k_ref, v_ref, o_ref, lse_ref, *, group, dilation, half, tq, kw,
                   heads, interleave):
    length = q_ref.shape[0]
    head0 = group * B_HEADS_PER_GROUP + pl.program_id(1) * heads
    scale = HEAD_DIM ** -0.5
    row = lax.broadcasted_iota(jnp.int32, (tq, kw), 0)
    col = lax.broadcasted_iota(jnp.int32, (tq, kw), 1)
    ones = jnp.ones((kw, HEAD_DIM), BF16)

    def tile(i, lanes, slope):
        m0 = pl.multiple_of(i * tq, tq)
        ks = pl.multiple_of(jnp.clip(m0 - half, 0, length - kw), half)
        q = q_ref[pl.ds(m0, tq), lanes]
        k = k_ref[pl.ds(ks, kw), lanes]
        v = v_ref[pl.ds(ks, kw), lanes]
        s = lax.dot_general(q, k, (((1,), (1,)), ((), ())), preferred_element_type=F32) * scale
        dist = jnp.abs(col - row + (ks - m0))
        bias = -slope * (dist * dilation).astype(F32)
        s = jnp.where(dist <= half, s + bias, NEG_INF)
        m = jnp.max(s, axis=-1, keepdims=True)
        p = jnp.exp(s - m).astype(BF16)
        ov = _dot(p, jnp.concatenate([v, ones], axis=1))
        denom = ov[:, HEAD_DIM:]
        o_ref[pl.ds(m0, tq), lanes] = (ov[:, :HEAD_DIM] / denom).astype(o_ref.dtype)
        lse_ref[pl.ds(m0, tq), lanes] = m + jnp.log(denom)

    for hh in range(heads):
        lanes = slice(hh * HEAD_DIM, (hh + 1) * HEAD_DIM)
        slope = slopes_ref[head0 + hh]

        def body(it, carry, lanes=lanes, slope=slope):
            for u in range(interleave):
                tile(it * interleave + u, lanes, slope)
            return carry

        lax.fori_loop(0, length // (tq * interleave), body, 0)


def _attn_b(qkv, slopes, group, *, bsz):
    window, dilation = B_PATTERNS[group]
    length = qkv.shape[-2] // (1 if dilation == 1 else bsz)
    half = (window // 2) // dilation
    tq = min(128, length)
    kw = min(length, tq + 2 * half)
    heads = 1 if dilation == 1 else B_HEADS_PER_GROUP
    steps = B_HEADS_PER_GROUP // heads
    width = heads * HEAD_DIM

    if dilation == 1:
        in_spec = lambda part: pl.BlockSpec((None, length, width), lambda bi, h, r, s: (bi, 0, part * steps + h))
        out_spec = pl.BlockSpec((None, length, width), lambda bi, h, r, s: (bi, 0, h))
        out_dims = (bsz, length, B_OUT_W)
    else:
        in_spec = lambda part: pl.BlockSpec((None, length, width), lambda bi, h, r, s: (r, bi, part * steps + h))
        out_spec = pl.BlockSpec((None, length, width), lambda bi, h, r, s: (r, bi, h))
        out_dims = (dilation, bsz * length, B_OUT_W)
    n_tiles = length // tq
    return pl.pallas_call(
        functools.partial(_attn_b_kernel, group=group, dilation=dilation, half=half, tq=tq, kw=kw,
                          heads=heads, interleave=min(8, n_tiles)),
        grid_spec=pltpu.PrefetchScalarGridSpec(
            num_scalar_prefetch=1,
            grid=(bsz, steps, dilation),
            in_specs=[in_spec(0), in_spec(1), in_spec(2)],
            out_specs=(out_spec, out_spec)),
        out_shape=(jax.ShapeDtypeStruct(out_dims, BF16), jax.ShapeDtypeStruct(out_dims, F32)),
        compiler_params=_cparams(("parallel", "parallel", "parallel"), VMEM_SMALL_MIB),
        name=f"attn_b{group}",
    )(slopes, qkv, qkv, qkv)


def _merge_b_kernel(o0_ref, l0_ref, o1_ref, l1_ref, o2_ref, l2_ref, yb_ref, on1_ref, ln1_ref, on2_ref, ln2_ref):
    for o_ref, l_ref, on_ref, ln_ref in ((o1_ref, l1_ref, on1_ref, ln1_ref),
                                         (o2_ref, l2_ref, on2_ref, ln2_ref)):
        dilation, rows = o_ref.shape[0], o_ref.shape[1]
        for c in range(on_ref.shape[0]):
            lanes = slice(c * HEAD_DIM, (c + 1) * HEAD_DIM)
            for r in range(dilation):
                on_ref[c, pl.ds(r, rows, stride=dilation), :] = o_ref[r, :, lanes].astype(F32)
                ln_ref[c, pl.ds(r, rows, stride=dilation), :] = l_ref[r, :, lanes]
    for c in range(on1_ref.shape[0]):
        lanes = slice(c * HEAD_DIM, (c + 1) * HEAD_DIM)
        l0, l1, l2 = l0_ref[:, lanes], ln1_ref[c], ln2_ref[c]
        m = jnp.maximum(jnp.maximum(l0, l1), l2)
        e0, e1, e2 = jnp.exp(l0 - m), jnp.exp(l1 - m), jnp.exp(l2 - m)
        tot = e0 + e1 + e2
        yb = (e0 / tot) * o0_ref[:, lanes].astype(F32) + (e1 / tot) * on1_ref[c] + (e2 / tot) * on2_ref[c]
        yb_ref[:, lanes] = yb.astype(BF16)


def _merge_b(o_l, *, bm):
    (o0, l0), (o1, l1), (o2, l2) = o_l
    n = o0.shape[0]
    row = pl.BlockSpec((bm, B_OUT_W), lambda i: (i, 0))

    def cls(arr):
        dil = arr.shape[0]
        return pl.BlockSpec((dil, bm // dil, B_OUT_W), lambda i: (0, i, 0))

    return pl.pallas_call(
        _merge_b_kernel,
        grid=(n // bm,),
        in_specs=[row, row, cls(o1), cls(l1), cls(o2), cls(l2)],
        out_specs=row,
        out_shape=jax.ShapeDtypeStruct((n, B_OUT_W), BF16),
        scratch_shapes=[pltpu.VMEM((B_HEADS_PER_GROUP, bm, HEAD_DIM), F32)] * 4,
        compiler_params=_cparams(("parallel",), VMEM_SMALL_MIB),
        name="merge_b",
    )(o0, l0, o1, l1, o2, l2)


def _branch_kernel(ya_ref, yb_ref, wa_ref, wb_ref, ga_ref, gb_ref, out_ref):
    a = _dot(ya_ref[...], wa_ref[...])
    bb = _dot(yb_ref[...], wb_ref[...])
    merged = ga_ref[...].astype(F32) * a + gb_ref[...].astype(F32) * bb
    out_ref[...] = merged.astype(out_ref.dtype)


def _branch(ya, yb, wa, wb, gates, *, bm, bn):
    n, d = ya.shape[0], wa.shape[1]
    row = lambda w: pl.BlockSpec((bm, w), lambda i, j: (i, 0))
    return pl.pallas_call(
        _branch_kernel,
        grid=(n // bm, d // bn),
        in_specs=[row(A_Q_W), row(B_OUT_W),
                  pl.BlockSpec((A_Q_W, bn), lambda i, j: (0, j)),
                  pl.BlockSpec((B_OUT_W, bn), lambda i, j: (0, j)),
                  pl.BlockSpec((bm, bn), lambda i, j: (i, j)),
                  pl.BlockSpec((bm, bn), lambda i, j: (i, d // bn + j))],
        out_specs=pl.BlockSpec((bm, bn), lambda i, j: (i, j)),
        out_shape=jax.ShapeDtypeStruct((n, d), BF16),
        compiler_params=_cparams(("parallel", "arbitrary"), VMEM_MM_MIB),
        name="branch_proj",
    )(ya, yb, wa, wb, gates, gates)


def _out_kernel(h_ref, a_ref, w_ref, o_ref):
    o_ref[...] = h_ref[...] + _dot(a_ref[...], w_ref[...])


def _out_proj(h, a, w, *, bm, bn):
    n, d = h.shape
    k = a.shape[1]
    return pl.pallas_call(
        _out_kernel,
        grid=(n // bm, d // bn),
        in_specs=[pl.BlockSpec((bm, bn), lambda i, j: (i, j)),
                  pl.BlockSpec((bm, k), lambda i, j: (i, 0)),
                  pl.BlockSpec((k, bn), lambda i, j: (0, j))],
        out_specs=pl.BlockSpec((bm, bn), lambda i, j: (i, j)),
        out_shape=jax.ShapeDtypeStruct((n, d), F32),
        compiler_params=_cparams(("parallel", "arbitrary"), VMEM_WIDE_MIB),
        name="out_proj",
    )(h, a, w)


def _layer(h, g_ffn1, w1_gate, w1_up, w1_down, g_mix, w_in, q_norm_a, k_norm_a,
           w_branch_a, w_branch_b, w_out, g_ffn2, w2_gate, w2_up, w2_down, g_next, *, bsz, last):
    n, d = h.shape
    t = n // bsz
    c = lambda w: w.astype(BF16)
    bm = _tile(n, 512)
    ffn = functools.partial(_ffn, bm=_tile(n, 1024), tf=_tile(w1_gate.shape[1], 256), io_rows=_tile(n, 128))

    later = [w2_gate, w2_up, w2_down, w_in, w_branch_a, w_branch_b, w_out]
    row_tiles, col_steps = n // ffn.keywords["bm"], w1_gate.shape[1] // ffn.keywords["tf"]
    rides = [_cast_block(w.shape, row_tiles, col_steps) is not None for w in later]
    h, u, *cast = ffn(h, g_ffn1, c(w1_gate), c(w1_up), c(w1_down), g_mix, emit_hidden=True,
                      cast_along=[w for w, ok in zip(later, rides) if ok])
    cast = iter(cast)
    *ffn2_w, w_in, w_branch_a, w_branch_b, w_out = [next(cast) if ok else c(w) for w, ok in zip(later, rides)]

    pm = _tile(t, 1024)
    a_w = A_Q_W + 2 * A_KV_W
    qp, kt, va = _proj_a(u, w_in, q_norm_a, k_norm_a, bsz=bsz, bm=pm)
    qkv_b = [_proj_cols(u, w_in, col0=a_w, width=B_GROUP_W, bm=pm, sigmoid=False,
                        name="proj_b0").reshape(bsz, t, B_GROUP_W)]
    for g in range(1, B_N_GROUPS):
        qkv_b.append(_proj_cls(u, w_in, col0=a_w + g * B_GROUP_W, width=B_GROUP_W, bm=pm,
                               dilation=B_PATTERNS[g][1], name=f"proj_b{g}"))
    gates = _proj_cols(u, w_in, col0=a_w + B_QKV_W, width=2 * d, bm=pm, sigmoid=True, name="proj_gates")

    ya = _attn_a(qp.reshape(bsz, t, A_Q_W), kt, va.reshape(bsz, t, A_KV_W), tq=_tile(t, 1024)).reshape(n, A_Q_W)

    slopes = jnp.exp2(-8.0 * jnp.arange(1, B_HEADS + 1, dtype=F32) / B_HEADS)
    o_l = [_attn_b(qkv_b[g], slopes, g, bsz=bsz) for g in range(B_N_GROUPS)]
    o_l[0] = tuple(a.reshape(n, B_OUT_W) for a in o_l[0])

    yb = _merge_b(o_l, bm=bm)
    merged = _branch(ya, yb, w_branch_a, w_branch_b, gates, bm=_tile(n, 1024), bn=_tile(d, 1024))
    h = _out_proj(h, merged, w_out, bm=_tile(n, 1024), bn=_tile(d, 1024))

    if last:
        return ffn(h, g_ffn2, *ffn2_w, g_next, emit_hidden=False, io_rows=_tile(n, 256))
    return ffn(h, g_ffn2, *ffn2_w, g_next, emit_hidden=True)[0]


def kernel(x, g_ffn1, w1_gate, w1_up, w1_down, g_mix, w_in, q_norm_a, k_norm_a, w_branch_a, w_branch_b,
           w_out, g_ffn2, w2_gate, w2_up, w2_down, g_final):
    bsz, t, d = x.shape
    depth = g_ffn1.shape[0]
    h = x.reshape(bsz * t, d)
    for l in range(depth):
        last = l == depth - 1
        h = _layer(h, g_ffn1[l], w1_gate[l], w1_up[l], w1_down[l], g_mix[l], w_in[l], q_norm_a[l],
                   k_norm_a[l], w_branch_a[l], w_branch_b[l], w_out[l], g_ffn2[l], w2_gate[l], w2_up[l],
                   w2_down[l], g_final if last else g_ffn1[l + 1], bsz=bsz, last=last)
    return h.reshape(bsz, t, d)
```

```python
import functools

import jax
import jax.numpy as jnp
from jax import lax
from jax.experimental import pallas as pl
from jax.experimental.pallas import tpu as pltpu

HEAD_DIM = 128
A_HEADS = 16
A_KV_HEADS = 4
A_GROUP = A_HEADS // A_KV_HEADS
B_PATTERNS = ((128, 1), (512, 4), (2048, 16))
B_HEADS_PER_GROUP = 8
B_N_GROUPS = len(B_PATTERNS)
B_HEADS = B_N_GROUPS * B_HEADS_PER_GROUP
A_Q_W = A_HEADS * HEAD_DIM
A_KV_W = A_KV_HEADS * HEAD_DIM
B_OUT_W = B_HEADS_PER_GROUP * HEAD_DIM
B_GROUP_W = 3 * B_OUT_W
B_QKV_W = B_N_GROUPS * B_GROUP_W
GRID_W = 64
ROPE_THETA = 10000.0
ROPE_AXIS_DIM = HEAD_DIM // 2
RMS_EPS = 1e-6
NEG_INF = -1e30

DOWN_CHUNK = 512
REGROUP_STRIDE = 4
FFN_MM_ROWS = 512
ATTN_A_ROWS = 128
PROJ_BN = 2 * A_KV_W
LANES = 128
BF16_SUBLANES = 16
MIB = 1024 * 1024
V7X_VMEM_MIB = 64
VMEM_FFN_MIB = V7X_VMEM_MIB - 4
VMEM_WIDE_MIB = V7X_VMEM_MIB - 8
VMEM_MM_MIB = V7X_VMEM_MIB - 16
VMEM_SMALL_MIB = V7X_VMEM_MIB - 24
BF16 = jnp.bfloat16
F32 = jnp.float32


def _cparams(sem, vmem_mib, flags=None):
    return pltpu.CompilerParams(dimension_semantics=sem, vmem_limit_bytes=vmem_mib * MIB, flags=flags)


def _tile(n, pref):
    t = min(n, pref)
    while n % t:
        t //= 2
    return t


def _rms(x, gain):
    y = x * lax.rsqrt(jnp.mean(x * x, axis=-1, keepdims=True) + RMS_EPS)
    return y * gain


def _dot(a, b):
    return jnp.dot(a, b, preferred_element_type=F32)


def _ffn_kernel(x_ref, gin_ref, wg_ref, wu_ref, wd_ref, gout_ref, *rest, emit_hidden, n_cast, n_io, n_f):
    cast_in, rest = rest[:n_cast], rest[n_cast:]
    outs, rest = rest[:2 if emit_hidden else 1], rest[2 if emit_hidden else 1:]
    cast_out, (xn_ref, acc_ref) = rest[:n_cast], rest[n_cast:]
    y_ref = outs[-1]
    io_rows = x_ref.shape[0]
    j = pl.program_id(1)
    first_mm, last_mm = n_io - 1, n_io + n_f - 2

    for src_ref, dst_ref in zip(cast_in, cast_out):
        dst_ref[...] = src_ref[...].astype(dst_ref.dtype)

    @pl.when(j <= first_mm)
    def _():
        rows = pl.ds(pl.multiple_of(j * io_rows, io_rows), io_rows)
        x = x_ref[...]
        xn_ref[rows, :] = _rms(x, gin_ref[...]).astype(BF16)
        acc_ref[rows, :] = x

    @pl.when((j >= first_mm) & (j <= last_mm))
    def _():
        for r0 in range(0, acc_ref.shape[0], FFN_MM_ROWS):
            rows = slice(r0, r0 + FFN_MM_ROWS)
            xn = xn_ref[rows, :]
            gate = _dot(xn, wg_ref[...])
            up = _dot(xn, wu_ref[...])
            act = ((0.25 * gate) * (1.0 + jnp.tanh(0.5 * gate)) * up).astype(BF16)
            for c in range(0, acc_ref.shape[1], DOWN_CHUNK):
                acc_ref[rows, c:c + DOWN_CHUNK] += _dot(act, wd_ref[:, c:c + DOWN_CHUNK])

    @pl.when(j >= last_mm)
    def _():
        rows = pl.ds(pl.multiple_of((j - last_mm) * io_rows, io_rows), io_rows)
        h = acc_ref[rows, :]
        if emit_hidden:
            outs[0][...] = h
        y_ref[...] = _rms(h, gout_ref[...]).astype(y_ref.dtype)


def _cast_block(shape, row_tiles, col_steps):
    rows, cols = shape
    if rows % row_tiles or (rows // row_tiles) % BF16_SUBLANES:
        return None
    for width in range(LANES, cols + 1, LANES):
        if cols % width == 0 and cols // width <= col_steps:
            return rows // row_tiles, width
    return None


def _ffn(x, g_in, wg, wu, wd, g_out, *, emit_hidden, bm, tf, io_rows, cast_along=()):
    n, d = x.shape
    f = wg.shape[1]
    n_io, n_f = bm // io_rows, f // tf
    assert bm % FFN_MM_ROWS == 0 and bm % io_rows == 0
    grid = (n // bm, n_f + 2 * (n_io - 1))
    fblk = lambda j: jnp.clip(j - (n_io - 1), 0, n_f - 1)
    vec = pl.BlockSpec((1, d), lambda i, j: (0, 0))
    in_specs = [pl.BlockSpec((io_rows, d), lambda i, j: (i * n_io + jnp.minimum(j, n_io - 1), 0)), vec,
                pl.BlockSpec((d, tf), lambda i, j: (0, fblk(j))),
                pl.BlockSpec((d, tf), lambda i, j: (0, fblk(j))),
                pl.BlockSpec((tf, d), lambda i, j: (fblk(j), 0)),
                vec]
    out_row = pl.BlockSpec((io_rows, d), lambda i, j: (i * n_io + jnp.clip(j - (n_io + n_f - 2), 0, n_io - 1), 0))
    if emit_hidden:
        out_shape = [jax.ShapeDtypeStruct((n, d), F32), jax.ShapeDtypeStruct((n, d), BF16)]
        out_specs = [out_row, out_row]
    else:
        out_shape = [jax.ShapeDtypeStruct((n, d), F32)]
        out_specs = [out_row]
    for w in cast_along:
        blk = _cast_block(w.shape, grid[0], n_f)
        last = w.shape[1] // blk[1] - 1
        spec = pl.BlockSpec(blk, lambda i, j, last=last: (i, jnp.minimum(fblk(j), last)))
        in_specs.append(spec)
        out_specs.append(spec)
        out_shape.append(jax.ShapeDtypeStruct(w.shape, BF16))
    outs = pl.pallas_call(
        functools.partial(_ffn_kernel, emit_hidden=emit_hidden, n_cast=len(cast_along), n_io=n_io, n_f=n_f),
        grid=grid, in_specs=in_specs, out_specs=out_specs, out_shape=out_shape,
        scratch_shapes=[pltpu.VMEM((bm, d), BF16), pltpu.VMEM((bm, d), F32)],
        compiler_params=_cparams(("parallel", "arbitrary"), VMEM_FFN_MIB),
        name="ffn_hidden" if emit_hidden else "ffn_final",
    )(x, g_in.reshape(1, d), wg, wu, wd, g_out.reshape(1, d), *cast_along)
    return outs if len(outs) > 1 else outs[0]


def _rope(y, cos, sin_signed, first_half):
    partner = jnp.where(first_half, pltpu.roll(y, HEAD_DIM - 32, axis=1), pltpu.roll(y, 32, axis=1))
    return y * cos + partner * sin_signed


def _head_pairs(u_ref, w_ref, width):
    u = u_ref[...]
    for c in range(0, width, 2 * HEAD_DIM):
        yield c // HEAD_DIM, _dot(u, w_ref[:, c:c + 2 * HEAD_DIM])


def _proj_q_kernel(u_ref, w_ref, cos_ref, sin_ref, gain_ref, q_ref, acc_ref):
    @pl.when(pl.program_id(0) == 0)
    def _():
        acc_ref[...] = jnp.zeros(acc_ref.shape, F32)

    cos, sin = cos_ref[...], sin_ref[...]
    first_half = (lax.broadcasted_iota(jnp.int32, cos.shape, 1) % 64) < 32
    scale = HEAD_DIM ** -0.5
    for h in range(q_ref.shape[1] // HEAD_DIM):
        lanes = slice(h * HEAD_DIM, (h + 1) * HEAD_DIM)
        y = _rope(_rms(acc_ref[:, lanes], gain_ref[...]), cos, sin, first_half)
        q_ref[:, lanes] = (y * scale).astype(BF16)
    acc_ref[...] = _dot(u_ref[...], w_ref[...])


def _proj_kv_kernel(u_ref, w_ref, cos_ref, sin_ref, gain_ref, kt_ref, v_ref):
    cos, sin = cos_ref[...], sin_ref[...]
    first_half = (lax.broadcasted_iota(jnp.int32, cos.shape, 1) % 64) < 32
    for h0, acc in _head_pairs(u_ref, w_ref, 2 * A_KV_W):
        for h in range(2):
            part = acc[:, h * HEAD_DIM:(h + 1) * HEAD_DIM]
            if h0 + h < A_KV_HEADS:
                y = _rope(_rms(part, gain_ref[...]), cos, sin, first_half)
                kt_ref[0, (h0 + h) * HEAD_DIM:(h0 + h + 1) * HEAD_DIM, :] = y.T.astype(BF16)
            else:
                hv = h0 + h - A_KV_HEADS
                v_ref[:, hv * HEAD_DIM:(hv + 1) * HEAD_DIM] = part.astype(BF16)


def _rope_tables(t):
    rows = t // GRID_W
    row_ids = jnp.repeat(jnp.arange(rows), GRID_W).astype(F32)
    col_ids = jnp.tile(jnp.arange(GRID_W), rows).astype(F32)
    inv = ROPE_THETA ** (-jnp.arange(0, ROPE_AXIS_DIM, 2, dtype=F32) / ROPE_AXIS_DIM)
    ang_r = row_ids[:, None] * inv[None, :]
    ang_c = col_ids[:, None] * inv[None, :]
    cos = jnp.concatenate([jnp.cos(ang_r)] * 2 + [jnp.cos(ang_c)] * 2, axis=-1)
    sin = jnp.concatenate([-jnp.sin(ang_r), jnp.sin(ang_r), -jnp.sin(ang_c), jnp.sin(ang_c)], axis=-1)
    return cos, sin


def _proj_a(u, w, q_gain, k_gain, *, bsz, bm):
    n, d = u.shape
    t = n // bsz
    assert t % bm == 0 and A_Q_W % PROJ_BN == 0
    tiles_per_seq = t // bm
    cos, sin = _rope_tables(t)
    u_spec = pl.BlockSpec((bm, d), lambda i, j: (i, 0))
    tab = pl.BlockSpec((bm, HEAD_DIM), lambda i, j: (i % tiles_per_seq, 0))
    vec = pl.BlockSpec((1, HEAD_DIM), lambda i, j: (0, 0))
    nq = A_Q_W // PROJ_BN
    steps = (n // bm) * nq
    cur = lambda s: jnp.minimum(s, steps - 1)
    fin = lambda s: jnp.maximum(s - 1, 0)
    fin_tab = pl.BlockSpec((bm, HEAD_DIM), lambda s: ((fin(s) // nq) % tiles_per_seq, 0))
    qp = pl.pallas_call(
        _proj_q_kernel,
        grid=(steps + 1,),
        in_specs=[pl.BlockSpec((bm, d), lambda s: (cur(s) // nq, 0)),
                  pl.BlockSpec((d, PROJ_BN), lambda s: (0, cur(s) % nq)),
                  fin_tab, fin_tab, pl.BlockSpec((1, HEAD_DIM), lambda s: (0, 0))],
        out_specs=pl.BlockSpec((bm, PROJ_BN), lambda s: (fin(s) // nq, fin(s) % nq)),
        out_shape=jax.ShapeDtypeStruct((n, A_Q_W), BF16),
        scratch_shapes=[pltpu.VMEM((bm, PROJ_BN), F32)],
        compiler_params=_cparams(("arbitrary",), VMEM_WIDE_MIB),
        name="proj_q",
    )(u, w, cos, sin, q_gain.reshape(1, HEAD_DIM))
    kt, v = pl.pallas_call(
        _proj_kv_kernel,
        grid=(n // bm, 1),
        in_specs=[u_spec, pl.BlockSpec((d, 2 * A_KV_W), lambda i, j: (0, A_Q_W // (2 * A_KV_W))), tab, tab, vec],
        out_specs=(pl.BlockSpec((1, A_KV_W, bm), lambda i, j: (i // tiles_per_seq, 0, i % tiles_per_seq)),
                   pl.BlockSpec((bm, A_KV_W), lambda i, j: (i, 0))),
        out_shape=(jax.ShapeDtypeStruct((bsz, A_KV_W, t), BF16), jax.ShapeDtypeStruct((n, A_KV_W), BF16)),
        compiler_params=_cparams(("parallel", "arbitrary"), VMEM_MM_MIB),
        name="proj_kv",
    )(u, w, cos, sin, k_gain.reshape(1, HEAD_DIM))
    return qp, kt, v


def _proj_cols_kernel(u_ref, w_ref, o_ref, *, sigmoid):
    acc = _dot(u_ref[...], w_ref[...])
    if sigmoid:
        acc = 0.5 * jnp.tanh(0.5 * acc) + 0.5
    o_ref[...] = acc.astype(o_ref.dtype)


def _proj_cols(u, w, *, col0, width, bm, sigmoid, name):
    n, d = u.shape
    bn = _tile(width, PROJ_BN)
    assert col0 % bn == 0
    return pl.pallas_call(
        functools.partial(_proj_cols_kernel, sigmoid=sigmoid),
        grid=(n // bm, width // bn),
        in_specs=[pl.BlockSpec((bm, d), lambda i, j: (i, 0)),
                  pl.BlockSpec((d, bn), lambda i, j: (0, col0 // bn + j))],
        out_specs=pl.BlockSpec((bm, bn), lambda i, j: (i, j)),
        out_shape=jax.ShapeDtypeStruct((n, width), BF16),
        compiler_params=_cparams(("parallel", "arbitrary"), VMEM_MM_MIB),
        name=name,
    )(u, w)


def _proj_cls_kernel(u_ref, w_ref, o_ref, acc_ref, mid_ref, *, dilation):
    acc = _dot(u_ref[...], w_ref[...])
    rows = o_ref.shape[1]
    bm = acc_ref.shape[1]
    for c in range(acc_ref.shape[0]):
        lanes = slice(c * HEAD_DIM, (c + 1) * HEAD_DIM)
        acc_ref[c] = acc[:, lanes]
        if dilation % (REGROUP_STRIDE * REGROUP_STRIDE):
            for r in range(dilation):
                o_ref[r, :, lanes] = acc_ref[c, pl.ds(r, rows, stride=dilation), :].astype(o_ref.dtype)
            continue
        s1, d2 = REGROUP_STRIDE, dilation // REGROUP_STRIDE
        part = bm // s1
        for r1 in range(s1):
            mid_ref[r1 * part:(r1 + 1) * part, :] = acc_ref[c, pl.ds(r1, part, stride=s1), :]
        for r1 in range(s1):
            for r2 in range(d2):
                o_ref[s1 * r2 + r1, :, lanes] = mid_ref[pl.ds(r1 * part + r2, rows, stride=d2), :].astype(
                    o_ref.dtype)


def _proj_cls(u, w, *, col0, width, bm, dilation, name):
    n, d = u.shape
    assert col0 % PROJ_BN == 0
    return pl.pallas_call(
        functools.partial(_proj_cls_kernel, dilation=dilation),
        grid=(n // bm, width // PROJ_BN),
        in_specs=[pl.BlockSpec((bm, d), lambda i, j: (i, 0)),
                  pl.BlockSpec((d, PROJ_BN), lambda i, j: (0, col0 // PROJ_BN + j))],
        out_specs=pl.BlockSpec((dilation, bm // dilation, PROJ_BN), lambda i, j: (0, i, j)),
        out_shape=jax.ShapeDtypeStruct((dilation, n // dilation, width), BF16),
        scratch_shapes=[pltpu.VMEM((PROJ_BN // HEAD_DIM, bm, HEAD_DIM), F32), pltpu.VMEM((bm, HEAD_DIM), F32)],
        compiler_params=_cparams(("parallel", "arbitrary"), VMEM_WIDE_MIB),
        name=name,
    )(u, w)


def _attn_a_kernel(q_ref, kt_ref, v_ref, o_ref, v1_ref):
    @pl.when(pl.program_id(2) == 0)
    def _():
        v1_ref[:, :HEAD_DIM] = v_ref[0]
        v1_ref[:, HEAD_DIM:] = jnp.ones((v1_ref.shape[0], HEAD_DIM), BF16)

    kt = kt_ref[0]
    v1 = v1_ref[...]
    for r0 in range(0, q_ref.shape[1], ATTN_A_ROWS):
        rows = slice(r0, r0 + ATTN_A_ROWS)
        for g in range(A_GROUP):
            sl = slice(g * HEAD_DIM, (g + 1) * HEAD_DIM)
            s = _dot(q_ref[0, rows, sl], kt)
            p = jnp.exp(s - jnp.max(s, axis=-1, keepdims=True)).astype(BF16)
            ov = _dot(p, v1)
            o_ref[0, rows, sl] = (ov[:, :HEAD_DIM] / ov[:, HEAD_DIM:]).astype(o_ref.dtype)


def _attn_a(qp, kt, v, *, tq):
    b, t, _ = qp.shape
    gw = A_GROUP * HEAD_DIM
    assert tq % ATTN_A_ROWS == 0
    return pl.pallas_call(
        _attn_a_kernel,
        grid=(b, A_KV_HEADS, t // tq),
        in_specs=[pl.BlockSpec((1, tq, gw), lambda bi, kv, i: (bi, i, kv)),
                  pl.BlockSpec((1, HEAD_DIM, t), lambda bi, kv, i: (bi, kv, 0)),
                  pl.BlockSpec((1, t, HEAD_DIM), lambda bi, kv, i: (bi, 0, kv))],
        out_specs=pl.BlockSpec((1, tq, gw), lambda bi, kv, i: (bi, i, kv)),
        out_shape=jax.ShapeDtypeStruct((b, t, A_Q_W), BF16),
        scratch_shapes=[pltpu.VMEM((t, 2 * HEAD_DIM), BF16)],
        compiler_params=_cparams(("parallel", "parallel", "arbitrary"), VMEM_MM_MIB),
        name="attn_a",
    )(qp, kt, v)


def _attn_b_kernel(slopes_ref, q_ref, k_ref, v_ref, o_ref, lse_ref, *, group, dilation, half, tq, kw,
                   heads, interleave):
    length = q_ref.shape[0]
    head0 = group * B_HEADS_PER_GROUP + pl.program_id(1) * heads
    scale = HEAD_DIM ** -0.5
    row = lax.broadcasted_iota(jnp.int32, (tq, kw), 0)
    col = lax.broadcasted_iota(jnp.int32, (tq, kw), 1)
    ones = jnp.ones((kw, HEAD_DIM), BF16)

    def tile(i, lanes, slope):
        m0 = pl.multiple_of(i * tq, tq)
        ks = pl.multiple_of(jnp.clip(m0 - half, 0, length - kw), half)
        q = q_ref[pl.ds(m0, tq), lanes]
        k = k_ref[pl.ds(ks, kw), lanes]
        v = v_ref[pl.ds(ks, kw), lanes]
        s = lax.dot_general(q, k, (((1,), (1,)), ((), ())), preferred_element_type=F32) * scale
        dist = jnp.abs(col - row + (ks - m0))
        bias = -slope * (dist * dilation).astype(F32)
        s = jnp.where(dist <= half, s + bias, NEG_INF)
        m = jnp.max(s, axis=-1, keepdims=True)
        p = jnp.exp(s - m).astype(BF16)
        ov = _dot(p, jnp.concatenate([v, ones], axis=1))
        denom = ov[:, HEAD_DIM:]
        o_ref[pl.ds(m0, tq), lanes] = (ov[:, :HEAD_DIM] / denom).astype(o_ref.dtype)
        lse_ref[pl.ds(m0, tq), lanes] = m + jnp.log(denom)

    for hh in range(heads):
        lanes = slice(hh * HEAD_DIM, (hh + 1) * HEAD_DIM)
        slope = slopes_ref[head0 + hh]

        def body(it, carry, lanes=lanes, slope=slope):
            for u in range(interleave):
                tile(it * interleave + u, lanes, slope)
            return carry

        lax.fori_loop(0, length // (tq * interleave), body, 0)


def _attn_b(qkv, slopes, group, *, bsz):
    window, dilation = B_PATTERNS[group]
    length = qkv.shape[-2] // (1 if dilation == 1 else bsz)
    half = (window // 2) // dilation
    tq = min(128, length)
    kw = min(length, tq + 2 * half)
    heads = 1 if dilation == 1 else B_HEADS_PER_GROUP
    steps = B_HEADS_PER_GROUP // heads
    width = heads * HEAD_DIM

    if dilation == 1:
        in_spec = lambda part: pl.BlockSpec((None, length, width), lambda bi, h, r, s: (bi, 0, part * steps + h))
        out_spec = pl.BlockSpec((None, length, width), lambda bi, h, r, s: (bi, 0, h))
        out_dims = (bsz, length, B_OUT_W)
    else:
        in_spec = lambda part: pl.BlockSpec((None, length, width), lambda bi, h, r, s: (r, bi, part * steps + h))
        out_spec = pl.BlockSpec((None, length, width), lambda bi, h, r, s: (r, bi, h))
        out_dims = (dilation, bsz * length, B_OUT_W)
    n_tiles = length // tq
    return pl.pallas_call(
        functools.partial(_attn_b_kernel, group=group, dilation=dilation, half=half, tq=tq, kw=kw,
                          heads=heads, interleave=min(8, n_tiles)),
        grid_spec=pltpu.PrefetchScalarGridSpec(
            num_scalar_prefetch=1,
            grid=(bsz, steps, dilation),
            in_specs=[in_spec(0), in_spec(1), in_spec(2)],
            out_specs=(out_spec, out_spec)),
        out_shape=(jax.ShapeDtypeStruct(out_dims, BF16), jax.ShapeDtypeStruct(out_dims, F32)),
        compiler_params=_cparams(("parallel", "parallel", "parallel"), VMEM_SMALL_MIB),
        name=f"attn_b{group}",
    )(slopes, qkv, qkv, qkv)


def _merge_b_kernel(o0_ref, l0_ref, o1_ref, l1_ref, o2_ref, l2_ref, yb_ref, on1_ref, ln1_ref, on2_ref, ln2_ref):
    for o_ref, l_ref, on_ref, ln_ref in ((o1_ref, l1_ref, on1_ref, ln1_ref),
                                         (o2_ref, l2_ref, on2_ref, ln2_ref)):
        dilation, rows = o_ref.shape[0], o_ref.shape[1]
        for c in range(on_ref.shape[0]):
            lanes = slice(c * HEAD_DIM, (c + 1) * HEAD_DIM)
            for r in range(dilation):
                on_ref[c, pl.ds(r, rows, stride=dilation), :] = o_ref[r, :, lanes].astype(F32)
                ln_ref[c, pl.ds(r, rows, stride=dilation), :] = l_ref[r, :, lanes]
    for c in range(on1_ref.shape[0]):
        lanes = slice(c * HEAD_DIM, (c + 1) * HEAD_DIM)
        l0, l1, l2 = l0_ref[:, lanes], ln1_ref[c], ln2_ref[c]
        m = jnp.maximum(jnp.maximum(l0, l1), l2)
        e0, e1, e2 = jnp.exp(l0 - m), jnp.exp(l1 - m), jnp.exp(l2 - m)
        tot = e0 + e1 + e2
        yb = (e0 / tot) * o0_ref[:, lanes].astype(F32) + (e1 / tot) * on1_ref[c] + (e2 / tot) * on2_ref[c]
        yb_ref[:, lanes] = yb.astype(BF16)


def _merge_b(o_l, *, bm):
    (o0, l0), (o1, l1), (o2, l2) = o_l
    n = o0.shape[0]
    row = pl.BlockSpec((bm, B_OUT_W), lambda i: (i, 0))

    def cls(arr):
        dil = arr.shape[0]
        return pl.BlockSpec((dil, bm // dil, B_OUT_W), lambda i: (0, i, 0))

    return pl.pallas_call(
        _merge_b_kernel,
        grid=(n // bm,),
        in_specs=[row, row, cls(o1), cls(l1), cls(o2), cls(l2)],
        out_specs=row,
        out_shape=jax.ShapeDtypeStruct((n, B_OUT_W), BF16),
        scratch_shapes=[pltpu.VMEM((B_HEADS_PER_GROUP, bm, HEAD_DIM), F32)] * 4,
        compiler_params=_cparams(("parallel",), VMEM_SMALL_MIB),
        name="merge_b",
    )(o0, l0, o1, l1, o2, l2)


def _branch_kernel(ya_ref, yb_ref, wa_ref, wb_ref, ga_ref, gb_ref, out_ref):
    a = _dot(ya_ref[...], wa_ref[...])
    bb = _dot(yb_ref[...], wb_ref[...])
    merged = ga_ref[...].astype(F32) * a + gb_ref[...].astype(F32) * bb
    out_ref[...] = merged.astype(out_ref.dtype)


def _branch(ya, yb, wa, wb, gates, *, bm, bn):
    n, d = ya.shape[0], wa.shape[1]
    row = lambda w: pl.BlockSpec((bm, w), lambda i, j: (i, 0))
    return pl.pallas_call(
        _branch_kernel,
        grid=(n // bm, d // bn),
        in_specs=[row(A_Q_W), row(B_OUT_W),
                  pl.BlockSpec((A_Q_W, bn), lambda i, j: (0, j)),
                  pl.BlockSpec((B_OUT_W, bn), lambda i, j: (0, j)),
                  pl.BlockSpec((bm, bn), lambda i, j: (i, j)),
                  pl.BlockSpec((bm, bn), lambda i, j: (i, d // bn + j))],
        out_specs=pl.BlockSpec((bm, bn), lambda i, j: (i, j)),
        out_shape=jax.ShapeDtypeStruct((n, d), BF16),
        compiler_params=_cparams(("parallel", "arbitrary"), VMEM_MM_MIB),
        name="branch_proj",
    )(ya, yb, wa, wb, gates, gates)


def _out_kernel(h_ref, a_ref, w_ref, o_ref):
    o_ref[...] = h_ref[...] + _dot(a_ref[...], w_ref[...])


def _out_proj(h, a, w, *, bm, bn):
    n, d = h.shape
    k = a.shape[1]
    return pl.pallas_call(
        _out_kernel,
        grid=(n // bm, d // bn),
        in_specs=[pl.BlockSpec((bm, bn), lambda i, j: (i, j)),
                  pl.BlockSpec((bm, k), lambda i, j: (i, 0)),
                  pl.BlockSpec((k, bn), lambda i, j: (0, j))],
        out_specs=pl.BlockSpec((bm, bn), lambda i, j: (i, j)),
        out_shape=jax.ShapeDtypeStruct((n, d), F32),
        compiler_params=_cparams(("parallel", "arbitrary"), VMEM_WIDE_MIB),
        name="out_proj",
    )(h, a, w)


def _layer(h, g_ffn1, w1_gate, w1_up, w1_down, g_mix, w_in, q_norm_a, k_norm_a,
           w_branch_a, w_branch_b, w_out, g_ffn2, w2_gate, w2_up, w2_down, g_next, *, bsz, last):
    n, d = h.shape
    t = n // bsz
    c = lambda w: w.astype(BF16)
    bm = _tile(n, 512)
    ffn = functools.partial(_ffn, bm=_tile(n, 1024), tf=_tile(w1_gate.shape[1], 256), io_rows=_tile(n, 128))

    later = [w2_gate, w2_up, w2_down, w_in, w_branch_a, w_branch_b, w_out]
    row_tiles, col_steps = n // ffn.keywords["bm"], w1_gate.shape[1] // ffn.keywords["tf"]
    rides = [_cast_block(w.shape, row_tiles, col_steps) is not None for w in later]
    h, u, *cast = ffn(h, g_ffn1, c(w1_gate), c(w1_up), c(w1_down), g_mix, emit_hidden=True,
                      cast_along=[w for w, ok in zip(later, rides) if ok])
    cast = iter(cast)
    *ffn2_w, w_in, w_branch_a, w_branch_b, w_out = [next(cast) if ok else c(w) for w, ok in zip(later, rides)]

    pm = _tile(t, 1024)
    a_w = A_Q_W + 2 * A_KV_W
    qp, kt, va = _proj_a(u, w_in, q_norm_a, k_norm_a, bsz=bsz, bm=pm)
    qkv_b = [_proj_cols(u, w_in, col0=a_w, width=B_GROUP_W, bm=pm, sigmoid=False,
                        name="proj_b0").reshape(bsz, t, B_GROUP_W)]
    for g in range(1, B_N_GROUPS):
        qkv_b.append(_proj_cls(u, w_in, col0=a_w + g * B_GROUP_W, width=B_GROUP_W, bm=pm,
                               dilation=B_PATTERNS[g][1], name=f"proj_b{g}"))
    gates = _proj_cols(u, w_in, col0=a_w + B_QKV_W, width=2 * d, bm=pm, sigmoid=True, name="proj_gates")

    ya = _attn_a(qp.reshape(bsz, t, A_Q_W), kt, va.reshape(bsz, t, A_KV_W), tq=_tile(t, 1024)).reshape(n, A_Q_W)

    slopes = jnp.exp2(-8.0 * jnp.arange(1, B_HEADS + 1, dtype=F32) / B_HEADS)
    o_l = [_attn_b(qkv_b[g], slopes, g, bsz=bsz) for g in range(B_N_GROUPS)]
    o_l[0] = tuple(a.reshape(n, B_OUT_W) for a in o_l[0])

    yb = _merge_b(o_l, bm=bm)
    merged = _branch(ya, yb, w_branch_a, w_branch_b, gates, bm=_tile(n, 1024), bn=_tile(d, 1024))
    h = _out_proj(h, merged, w_out, bm=_tile(n, 1024), bn=_tile(d, 1024))

    if last:
        return ffn(h, g_ffn2, *ffn2_w, g_next, emit_hidden=False, io_rows=_tile(n, 256))
    return ffn(h, g_ffn2, *ffn2_w, g_next, emit_hidden=True)[0]


def kernel(x, g_ffn1, w1_gate, w1_up, w1_down, g_mix, w_in, q_norm_a, k_norm_a, w_branch_a, w_branch_b,
           w_out, g_ffn2, w2_gate, w2_up, w2_down, g_final):
    bsz, t, d = x.shape
    depth = g_ffn1.shape[0]
    h = x.reshape(bsz * t, d)
    for l in range(depth):
        last = l == depth - 1
        h = _layer(h, g_ffn1[l], w1_gate[l], w1_up[l], w1_down[l], g_mix[l], w_in[l], q_norm_a[l],
                   k_norm_a[l], w_branch_a[l], w_branch_b[l], w_out[l], g_ffn2[l], w2_gate[l], w2_up[l],
                   w2_down[l], g_final if last else g_ffn1[l + 1], bsz=bsz, last=last)
    return h.reshape(bsz, t, d)
```
